```python
import math
import jax
import jax.numpy as jnp
from jax import lax
import numpy as np

D_MODEL = 2048
BATCH = 32
SEQ = 256
DEPTH = 4
DEC_BATCH = 4
DEC_SEQ = 1024
PAST_LEN = 256

GRID_W = 64
N_HEADS = 8
N_KV_HEADS = 2
HEAD_DIM = 128
AXIS_DIM = HEAD_DIM // 2
ROPE_THETA = 10000.0
Q_BLOCK = 128
D_ATTN = N_HEADS * HEAD_DIM
D_KV = N_KV_HEADS * HEAD_DIM
D_RNN = D_MODEL // 2
N_RNN_BLOCKS = 8
RNN_BLOCK = D_RNN // N_RNN_BLOCKS
RNN_CONV_W = 4
LRU_C = 8.0
D_SSM = D_MODEL // 2
SSM_GROUP = 16
N_SSM_GROUPS = D_SSM // SSM_GROUP
SSM_STATE = 64
D_FF = 2 * D_MODEL
FFN_CONV_W = 3
N_BRANCH = 3
N_MOD = 6
EPS = 1e-6
SPLIT_SIZES = (D_ATTN, D_KV, D_KV, D_RNN, D_RNN, D_SSM, N_BRANCH * D_MODEL)
D_IN = D_ATTN + 2 * D_KV + 2 * D_RNN + D_SSM + N_BRANCH * D_MODEL

kernel_name = 'hybrid_dit_attn_rglru_s5_step'

F32 = jnp.float32


def rms_norm(x, g):
    xf = x.astype(F32)
    y = xf * lax.rsqrt(jnp.mean(xf * xf, axis=-1, keepdims=True) + EPS)
    return (y * g.astype(F32)).astype(x.dtype)


def dwconv_centred(x, w, b):
    width = w.shape[0]
    left = width // 2
    y = lax.conv_general_dilated(
        x, w[:, None, :].astype(x.dtype), window_strides=(1,),
        padding=[(left, width - 1 - left)],
        dimension_numbers=('NWC', 'WIO', 'NWC'),
        feature_group_count=x.shape[-1])
    return y + b.astype(x.dtype)


def linear_scan(a, b, h0, reverse):
    def comb(e1, e2):
        a1, b1 = e1
        a2, b2 = e2
        return a1 * a2, a2 * b1 + b2
    a_cum, b_cum = lax.associative_scan(comb, (a, b), reverse=reverse, axis=1)
    h = a_cum * h0[:, None] + b_cum
    final = h[:, 0] if reverse else h[:, -1]
    return h, final


def complex_linear_scan(ar, ai, br, bi, h0r, h0i, reverse):
    def comb(e1, e2):
        a1r, a1i, b1r, b1i = e1
        a2r, a2i, b2r, b2i = e2
        return (a2r * a1r - a2i * a1i, a2r * a1i + a2i * a1r,
                a2r * b1r - a2i * b1i + b2r, a2r * b1i + a2i * b1r + b2i)
    a_r, a_i, b_r, b_i = lax.associative_scan(comb, (ar, ai, br, bi), reverse=reverse, axis=1)
    h0r = h0r[:, None]
    h0i = h0i[:, None]
    hr = a_r * h0r - a_i * h0i + b_r
    hi = a_r * h0i + a_i * h0r + b_i
    if reverse:
        return hr, hi, hr[:, 0], hi[:, 0]
    return hr, hi, hr[:, -1], hi[:, -1]


def axial_rope(n_tokens):
    rows = n_tokens // GRID_W
    t_row = jnp.repeat(jnp.arange(rows, dtype=F32), GRID_W)
    t_col = (jnp.arange(n_tokens) % GRID_W).astype(F32)
    inv = 1.0 / (ROPE_THETA ** (jnp.arange(0, AXIS_DIM, 2, dtype=F32) / AXIS_DIM))
    ang = jnp.stack([t_row[:, None] * inv, t_col[:, None] * inv], axis=0)
    return jnp.cos(ang), jnp.sin(ang)


def apply_rope(x, cos, sin):
    xf = x.astype(F32)
    half = AXIS_DIM // 2
    parts = []
    for ax in range(2):
        seg = xf[..., ax * AXIS_DIM:(ax + 1) * AXIS_DIM]
        x1, x2 = seg[..., :half], seg[..., half:]
        cs = cos[ax][:, None, :]
        sn = sin[ax][:, None, :]
        parts.append(x1 * cs - x2 * sn)
        parts.append(x2 * cs + x1 * sn)
    return jnp.concatenate(parts, axis=-1).astype(x.dtype)


def block_attention(q, k, v):
    n, sq, _, dh = q.shape
    n_blk = sq // Q_BLOCK
    grp = N_HEADS // N_KV_HEADS
    qb = q.reshape(n, n_blk, Q_BLOCK, N_KV_HEADS, grp, dh).transpose(1, 0, 2, 3, 4, 5)
    scale = dh ** -0.5

    def one_block(q_blk):
        s = jnp.einsum('bqkgd,bskd->bkgqs', q_blk, k, preferred_element_type=F32) * scale
        p = jax.nn.softmax(s, axis=-1).astype(v.dtype)
        return jnp.einsum('bkgqs,bskd->bqkgd', p, v)

    o = lax.map(one_block, qb)
    return o.transpose(1, 0, 2, 3, 4, 5).reshape(n, sq, N_HEADS * dh)


def rg_lru_bidir(x, h0, gate_w, gate_b, lam):
    n, L, _ = x.shape
    xf = x.astype(F32)
    xb = xf.reshape(n, L, N_RNN_BLOCKS, RNN_BLOCK)
    hs, finals = [], []
    for d in range(2):
        gates = jnp.einsum('blkc,jkcd->jblkd', xb, gate_w[d].astype(F32)).reshape(2, n, L, D_RNN)
        gates = gates + gate_b[d].astype(F32)[:, None, None, :]
        r = jax.nn.sigmoid(gates[0])
        i = jax.nn.sigmoid(gates[1])
        log_a = -LRU_C * r * jax.nn.softplus(-lam[d].astype(F32))
        a = jnp.exp(log_a)
        b = jnp.sqrt(-jnp.expm1(2.0 * log_a)) * (i * xf)
        h, hf = linear_scan(a, b, h0[:, d].astype(F32), reverse=(d == 1))
        hs.append(h)
        finals.append(hf)
    return (hs[0] + hs[1]).astype(x.dtype), jnp.stack(finals, axis=1).astype(x.dtype)


def s5_bidir(u, h0_re, h0_im, a_re, a_im, log_dt, b_re, b_im, c_re, c_im, d_skip):
    n, L, _ = u.shape
    uf = u.astype(F32).reshape(n, L, N_SSM_GROUPS, SSM_GROUP)
    ys, fr, fi = [], [], []
    for d in range(2):
        lam_r = jnp.minimum(a_re[d].astype(F32), -1e-4)
        lam_i = a_im[d].astype(F32)
        dt = jnp.exp(log_dt[d].astype(F32))[:, None]
        mag = jnp.exp(lam_r * dt)
        ab_r = mag * jnp.cos(lam_i * dt)
        ab_i = mag * jnp.sin(lam_i * dt)
        den = lam_r * lam_r + lam_i * lam_i
        q_r = ((ab_r - 1.0) * lam_r + ab_i * lam_i) / den
        q_i = (ab_i * lam_r - (ab_r - 1.0) * lam_i) / den
        br = b_re[d].astype(F32)
        bi = b_im[d].astype(F32)
        bb_r = q_r[..., None] * br - q_i[..., None] * bi
        bb_i = q_r[..., None] * bi + q_i[..., None] * br
        bu_r = jnp.einsum('gph,blgh->blgp', bb_r, uf)
        bu_i = jnp.einsum('gph,blgh->blgp', bb_i, uf)
        hr, hi, f_r, f_i = complex_linear_scan(
            jnp.broadcast_to(ab_r, bu_r.shape), jnp.broadcast_to(ab_i, bu_r.shape),
            bu_r, bu_i, h0_re[:, d].astype(F32), h0_im[:, d].astype(F32), reverse=(d == 1))
        y = (jnp.einsum('ghp,blgp->blgh', c_re[d].astype(F32), hr)
             - jnp.einsum('ghp,blgp->blgh', c_im[d].astype(F32), hi))
        ys.append(y)
        fr.append(f_r)
        fi.append(f_i)
    y = (ys[0] + ys[1]).reshape(n, L, D_SSM) + d_skip.astype(F32) * u.astype(F32)
    return (y.astype(u.dtype), jnp.stack(fr, axis=1).astype(u.dtype),
            jnp.stack(fi, axis=1).astype(u.dtype))


def trunk_layer(x, mod, p, rope, ctx):
    n, L, _ = x.shape
    shift1, scale1, gate1, shift2, scale2, gate2 = [mod[:, :, i] for i in range(N_MOD)]
    h = rms_norm(x, p['g_mix']) * (1 + scale1) + shift1
    z = h @ p['w_in']
    offs = np.cumsum(SPLIT_SIZES)[:-1].tolist()
    q, k, v, xr, xg, us, gl = jnp.split(z, offs, axis=-1)
    q = rms_norm(q.reshape(n, L, N_HEADS, HEAD_DIM), p['q_gain'])
    k = rms_norm(k.reshape(n, L, N_KV_HEADS, HEAD_DIM), p['k_gain'])
    v = v.reshape(n, L, N_KV_HEADS, HEAD_DIM)
    if ctx is None:
        k_all, v_all = k, v
        h0_lru = jnp.zeros((n, 2, D_RNN), x.dtype)
        h0_sr = jnp.zeros((n, 2, N_SSM_GROUPS, SSM_STATE), x.dtype)
        h0_si = h0_sr
    else:
        ck, cv, h0_lru, h0_sr, h0_si = ctx
        q = apply_rope(q, rope[0], rope[1])
        k = apply_rope(k, rope[0], rope[1])
        k_all = jnp.concatenate([ck.astype(k.dtype), k], axis=1)
        v_all = jnp.concatenate([cv.astype(v.dtype), v], axis=1)
    attn = block_attention(q, k_all, v_all)
    xr = dwconv_centred(xr, p['rnn_conv_w'], p['rnn_conv_b'])
    lru, lru_f = rg_lru_bidir(xr, h0_lru, p['lru_gate_w'], p['lru_gate_b'], p['lru_lambda'])
    rec = jax.nn.gelu(xg) * lru
    ssm, s5_fr, s5_fi = s5_bidir(us, h0_sr, h0_si, p['s5_a_re'], p['s5_a_im'], p['s5_log_dt'],
                                 p['s5_b_re'], p['s5_b_im'], p['s5_c_re'], p['s5_c_im'], p['s5_d'])
    ssm = jax.nn.gelu(ssm)
    ssm = ssm * jax.nn.sigmoid(ssm @ p['glu_w'] + p['glu_b'])
    g = jax.nn.sigmoid(gl).reshape(n, L, N_BRANCH, D_MODEL)
    merged = (g[:, :, 0] * (attn @ p['w_pa']) + g[:, :, 1] * (rec @ p['w_pr'])
              + g[:, :, 2] * (ssm @ p['w_ps']))
    x = x + gate1 * (merged @ p['w_o'])
    h2 = rms_norm(x, p['g_ffn']) * (1 + scale2) + shift2
    a, b = jnp.split(h2 @ p['w_up'], 2, axis=-1)
    a = dwconv_centred(a, p['ffn_conv_w'], p['ffn_conv_b'])
    x = x + gate2 * ((jax.nn.gelu(a) * b) @ p['w_down'])
    return x, (k, v, lru_f, s5_fr, s5_fi)


def setup_inputs(seed: int = 0) -> dict:
    key = jax.random.key(seed)
    ks = iter(jax.random.split(key, 64))

    def nrm(shape, s):
        return jax.random.normal(next(ks), shape, F32) * s

    G, P, H = N_SSM_GROUPS, SSM_STATE, SSM_GROUP
    lam_u = jax.random.uniform(next(ks), (DEPTH, 2, D_RNN), F32, 0.9, 0.999)
    lru_lambda = jnp.log(lam_u) - jnp.log1p(-lam_u)
    s5_a_im = math.pi * jnp.arange(P, dtype=F32) + nrm((DEPTH, 2, G, P), 0.01)
    s5_log_dt = jax.random.uniform(next(ks), (DEPTH, 2, G), F32, math.log(1e-3), math.log(1e-1))
    return {
        'x_prompt': nrm((BATCH, SEQ, D_MODEL), 1.0),
        'x_sample': nrm((DEC_BATCH, DEC_SEQ, D_MODEL), 1.0),
        'c': nrm((DEC_BATCH, D_MODEL), 1.0),
        'cache_k': nrm((DEC_BATCH, DEPTH, PAST_LEN, N_KV_HEADS, HEAD_DIM), 1.0),
        'cache_v': nrm((DEC_BATCH, DEPTH, PAST_LEN, N_KV_HEADS, HEAD_DIM), 1.0),
        'state_lru': nrm((DEC_BATCH, DEPTH, 2, D_RNN), 0.5),
        'state_s5_re': nrm((DEC_BATCH, DEPTH, 2, G, P), 0.5),
        'state_s5_im': nrm((DEC_BATCH, DEPTH, 2, G, P), 0.5),
        'c_ctx': nrm((D_MODEL,), 1.0),
        'w_mod': nrm((DEPTH, D_MODEL, N_MOD * D_MODEL), D_MODEL ** -0.5),
        'b_mod': nrm((DEPTH, N_MOD * D_MODEL), 0.02),
        'g_mix': 1.0 + nrm((DEPTH, D_MODEL), 0.02),
        'w_in': nrm((DEPTH, D_MODEL, D_IN), D_MODEL ** -0.5),
        'q_gain': 1.0 + nrm((DEPTH, HEAD_DIM), 0.02),
        'k_gain': 1.0 + nrm((DEPTH, HEAD_DIM), 0.02),
        'rnn_conv_w': nrm((DEPTH, RNN_CONV_W, D_RNN), RNN_CONV_W ** -0.5),
        'rnn_conv_b': nrm((DEPTH, D_RNN), 0.01),
        'lru_gate_w': nrm((DEPTH, 2, 2, N_RNN_BLOCKS, RNN_BLOCK, RNN_BLOCK), RNN_BLOCK ** -0.5),
        'lru_gate_b': nrm((DEPTH, 2, 2, D_RNN), 0.1),
        'lru_lambda': lru_lambda,
        's5_a_re': -0.5 + nrm((DEPTH, 2, G, P), 0.01),
        's5_a_im': s5_a_im,
        's5_log_dt': s5_log_dt,
        's5_b_re': nrm((DEPTH, 2, G, P, H), (2 * H) ** -0.5),
        's5_b_im': nrm((DEPTH, 2, G, P, H), (2 * H) ** -0.5),
        's5_c_re': nrm((DEPTH, 2, G, H, P), (2 * P) ** -0.5),
        's5_c_im': nrm((DEPTH, 2, G, H, P), (2 * P) ** -0.5),
        's5_d': nrm((DEPTH, D_SSM), 1.0),
        'glu_w': nrm((DEPTH, D_SSM, D_SSM), D_SSM ** -0.5),
        'glu_b': nrm((DEPTH, D_SSM), 0.02),
        'w_pa': nrm((DEPTH, D_ATTN, D_MODEL), D_ATTN ** -0.5),
        'w_pr': nrm((DEPTH, D_RNN, D_MODEL), D_RNN ** -0.5),
        'w_ps': nrm((DEPTH, D_SSM, D_MODEL), D_SSM ** -0.5),
        'w_o': nrm((DEPTH, D_MODEL, D_MODEL), D_MODEL ** -0.5),
        'g_ffn': 1.0 + nrm((DEPTH, D_MODEL), 0.02),
        'w_up': nrm((DEPTH, D_MODEL, 2 * D_FF), D_MODEL ** -0.5),
        'ffn_conv_w': nrm((DEPTH, FFN_CONV_W, D_FF), FFN_CONV_W ** -0.5),
        'ffn_conv_b': nrm((DEPTH, D_FF), 0.01),
        'w_down': nrm((DEPTH, D_FF, D_MODEL), D_FF ** -0.5),
    }


def reference(x_prompt, x_sample, c, cache_k, cache_v, state_lru, state_s5_re, state_s5_im,
              c_ctx, w_mod, b_mod, g_mix, w_in, q_gain, k_gain, rnn_conv_w, rnn_conv_b,
              lru_gate_w, lru_gate_b, lru_lambda, s5_a_re, s5_a_im, s5_log_dt, s5_b_re, s5_b_im,
              s5_c_re, s5_c_im, s5_d, glu_w, glu_b, w_pa, w_pr, w_ps, w_o, g_ffn, w_up,
              ffn_conv_w, ffn_conv_b, w_down):
    rope = axial_rope(x_sample.shape[1])
    yp = x_prompt
    ys = x_sample
    new_k, new_v, new_lru, new_sr, new_si = [], [], [], [], []
    for l in range(DEPTH):
        p = {
            'g_mix': g_mix[l], 'w_in': w_in[l], 'q_gain': q_gain[l], 'k_gain': k_gain[l],
            'rnn_conv_w': rnn_conv_w[l], 'rnn_conv_b': rnn_conv_b[l],
            'lru_gate_w': lru_gate_w[l], 'lru_gate_b': lru_gate_b[l], 'lru_lambda': lru_lambda[l],
            's5_a_re': s5_a_re[l], 's5_a_im': s5_a_im[l], 's5_log_dt': s5_log_dt[l],
            's5_b_re': s5_b_re[l], 's5_b_im': s5_b_im[l], 's5_c_re': s5_c_re[l],
            's5_c_im': s5_c_im[l], 's5_d': s5_d[l], 'glu_w': glu_w[l], 'glu_b': glu_b[l],
            'w_pa': w_pa[l], 'w_pr': w_pr[l], 'w_ps': w_ps[l], 'w_o': w_o[l],
            'g_ffn': g_ffn[l], 'w_up': w_up[l], 'ffn_conv_w': ffn_conv_w[l],
            'ffn_conv_b': ffn_conv_b[l], 'w_down': w_down[l],
        }
        mod_ctx = (jax.nn.silu(c_ctx)[None] @ w_mod[l] + b_mod[l]).reshape(1, 1, N_MOD, D_MODEL)
        mod_lat = (jax.nn.silu(c) @ w_mod[l] + b_mod[l]).reshape(-1, 1, N_MOD, D_MODEL)
        yp, st = trunk_layer(yp, mod_ctx, p, None, None)
        new_k.append(st[0])
        new_v.append(st[1])
        new_lru.append(st[2])
        new_sr.append(st[3])
        new_si.append(st[4])
        ctx = (cache_k[:, l], cache_v[:, l], state_lru[:, l], state_s5_re[:, l], state_s5_im[:, l])
        ys, _ = trunk_layer(ys, mod_lat, p, rope, ctx)
    new_cache_k = jnp.stack(new_k, axis=1)
    new_cache_v = jnp.stack(new_v, axis=1)
    new_state_lru = jnp.stack(new_lru, axis=1)
    new_state_s5_re = jnp.stack(new_sr, axis=1)
    new_state_s5_im = jnp.stack(new_si, axis=1)
    return (yp, ys, new_cache_k, new_cache_v, new_state_lru, new_state_s5_re, new_state_s5_im)
```

```python
import functools
import math

import jax
import jax.numpy as jnp
from jax import lax
from jax.experimental import pallas as pl
from jax.experimental.pallas import tpu as pltpu

F32 = jnp.float32
BF = jnp.bfloat16

D_MODEL = 2048
BATCH = 32
SEQ = 256
DEPTH = 4
DEC_BATCH = 4
DEC_SEQ = 1024
PAST_LEN = 256
GRID_W = 64
N_HEADS = 8
N_KV_HEADS = 2
HEAD_DIM = 128
AXIS_DIM = HEAD_DIM // 2
ROPE_THETA = 10000.0
D_ATTN = N_HEADS * HEAD_DIM
D_KV = N_KV_HEADS * HEAD_DIM
D_RNN = D_MODEL // 2
RNN_BLOCK = 128
N_RNN_BLOCKS = D_RNN // RNN_BLOCK
RNN_CONV_W = 4
LRU_C = 8.0
D_SSM = D_MODEL // 2
SSM_GROUP = 16
N_SSM_GROUPS = D_SSM // SSM_GROUP
SSM_STATE = 64
D_FF = 2 * D_MODEL
N_MOD = 6
EPS = 1e-6
D_IN = D_ATTN + 2 * D_KV + 2 * D_RNN + D_SSM + 3 * D_MODEL

OFF_Q = 0
OFF_K = D_ATTN
OFF_V = OFF_K + D_KV
OFF_XR = OFF_V + D_KV
OFF_XG = OFF_XR + D_RNN
OFF_US = OFF_XG + D_RNN
OFF_GL = OFF_US + D_SSM

T_CTX = BATCH * SEQ
T_LAT = DEC_BATCH * DEC_SEQ
T_ALL = T_CTX + T_LAT
MOD_ROWS = 8

SUBLANES = 8
LANES = 128
VMEM_LIMIT = 56 * 1024 * 1024

S5_SEQS = SUBLANES // 2
S5_GROUPS = LANES // SSM_GROUP
S5_TILES = N_SSM_GROUPS // S5_GROUPS
S5_STATES = S5_GROUPS * SSM_STATE
S5_CHUNK = 128


def _params(n_axes):
    return pltpu.CompilerParams(dimension_semantics=("arbitrary",) * n_axes,
                                vmem_limit_bytes=VMEM_LIMIT)


def _mod_row(first_token):
    return jnp.where(first_token < T_CTX, 0, 1 + jnp.maximum(first_token - T_CTX, 0) // DEC_SEQ)


def _gelu(x):
    return jax.nn.gelu(x)


def _mod_kernel(c_ref, w_ref, b_ref, o_ref):
    c = c_ref[...]
    s = (c * jax.nn.sigmoid(c)).astype(BF)
    o_ref[...] = jnp.dot(s, w_ref[...].astype(BF), preferred_element_type=F32) + b_ref[...]


def _modulation(cvec, w_mod, b_mod):
    tn = 1536
    n_out = N_MOD * D_MODEL
    return pl.pallas_call(
        _mod_kernel,
        grid=(DEPTH, n_out // tn),
        in_specs=[pl.BlockSpec((MOD_ROWS, D_MODEL), lambda l, j: (0, 0)),
                  pl.BlockSpec((None, D_MODEL, tn), lambda l, j: (l, 0, j)),
                  pl.BlockSpec((None, 1, tn), lambda l, j: (l, 0, j))],
        out_specs=pl.BlockSpec((None, MOD_ROWS, tn), lambda l, j: (l, 0, j)),
        out_shape=jax.ShapeDtypeStruct((DEPTH, MOD_ROWS, n_out), F32),
        compiler_params=_params(2),
        name="modulation",
    )(cvec, w_mod, b_mod.reshape(DEPTH, 1, n_out))


def _norm_kernel(x_ref, g_ref, sc_ref, sh_ref, o_ref, *, tm):
    r = _mod_row(pl.program_id(0) * tm)
    x = x_ref[...]
    y = x * lax.rsqrt(jnp.mean(x * x, axis=-1, keepdims=True) + EPS)
    y = y * g_ref[...]
    sc = sc_ref[pl.ds(r, 1), :]
    sh = sh_ref[pl.ds(r, 1), :]
    o_ref[...] = (y * (1.0 + sc) + sh).astype(BF)


def _norm_mod(x, gain, mod, layer, shift_idx, scale_idx):
    tm = 512
    return pl.pallas_call(
        functools.partial(_norm_kernel, tm=tm),
        grid=(T_ALL // tm,),
        in_specs=[pl.BlockSpec((tm, D_MODEL), lambda i: (i, 0)),
                  pl.BlockSpec((None, 1, D_MODEL), lambda i: (layer, 0, 0)),
                  pl.BlockSpec((None, MOD_ROWS, D_MODEL), lambda i: (layer, 0, scale_idx)),
                  pl.BlockSpec((None, MOD_ROWS, D_MODEL), lambda i: (layer, 0, shift_idx))],
        out_specs=pl.BlockSpec((tm, D_MODEL), lambda i: (i, 0)),
        out_shape=jax.ShapeDtypeStruct((T_ALL, D_MODEL), BF),
        compiler_params=_params(1),
        name="norm_mod",
    )(x, gain.reshape(DEPTH, 1, D_MODEL), mod, mod)


def _ws_kernel(*refs, n_a, n_w, n_extra, n_out, lhs_of, epilogue, tm):
    a_refs = refs[:n_a]
    w_refs = refs[n_a:n_a + n_w]
    extra = refs[n_a + n_w:n_a + n_w + n_extra]
    outs = refs[n_a + n_w + n_extra:n_a + n_w + n_extra + n_out]
    wbf = refs[n_a + n_w + n_extra + n_out:]
    j = pl.program_id(0)
    i = pl.program_id(1)

    @pl.when(i == 0)
    def _():
        for k in range(n_w):
            wbf[k][...] = w_refs[k][...].astype(BF)

    accs = [jnp.dot(a_refs[lhs_of[k]][...], wbf[k][...], preferred_element_type=F32)
            for k in range(n_w)]
    epilogue(accs, extra, outs, j, i * tm)


def _ws_matmul(a_list, w_list, extra, out_shapes, out_specs, epilogue, *, lhs_of, tm, tn, n_tiles, name):
    in_specs, args, scratch = [], [], []
    for a in a_list:
        in_specs.append(pl.BlockSpec((tm, a.shape[1]), lambda j, i: (i, 0)))
        args.append(a)
    for w, layer, off in w_list:
        kdim = w.shape[1]
        in_specs.append(pl.BlockSpec((None, kdim, tn), lambda j, i, layer=layer, off=off: (layer, 0, off + j)))
        args.append(w)
        scratch.append(pltpu.VMEM((kdim, tn), BF))
    for arr, spec in extra:
        in_specs.append(spec)
        args.append(arr)
    kern = functools.partial(_ws_kernel, n_a=len(a_list), n_w=len(w_list), n_extra=len(extra),
                             n_out=len(out_shapes), lhs_of=lhs_of, epilogue=epilogue, tm=tm)
    return pl.pallas_call(
        kern,
        grid=(n_tiles, T_ALL // tm),
        in_specs=in_specs,
        out_specs=out_specs,
        out_shape=out_shapes,
        scratch_shapes=scratch,
        compiler_params=_params(2),
        name=name,
    )(*args)


def _in_proj(h, w_in, layer):
    tm, tn = 512, 1536

    def epilogue(accs, extra, outs, j, row0):
        outs[0][...] = accs[0]

    return _ws_matmul([h], [(w_in, layer, 0)], [],
                      [jax.ShapeDtypeStruct((T_ALL, D_IN), F32)],
                      [pl.BlockSpec((tm, tn), lambda j, i: (i, j))],
                      epilogue, lhs_of=(0,), tm=tm, tn=tn, n_tiles=D_IN // tn, name="in_proj")[0]


def _merge(attn, rec, ssm, z, w_pa, w_pr, w_ps, layer):
    tm, tn = 512, 512
    gl0 = OFF_GL // tn

    def epilogue(accs, extra, outs, j, row0):
        m = jax.nn.sigmoid(extra[0][...]) * accs[0]
        m = m + jax.nn.sigmoid(extra[1][...]) * accs[1]
        m = m + jax.nn.sigmoid(extra[2][...]) * accs[2]
        outs[0][...] = m.astype(BF)

    gates = [(z, pl.BlockSpec((tm, tn), lambda j, i, b=b: (i, gl0 + b * (D_MODEL // tn) + j)))
             for b in range(3)]
    return _ws_matmul([attn, rec, ssm], [(w_pa, layer, 0), (w_pr, layer, 0), (w_ps, layer, 0)], gates,
                      [jax.ShapeDtypeStruct((T_ALL, D_MODEL), BF)],
                      [pl.BlockSpec((tm, tn), lambda j, i: (i, j))],
                      epilogue, lhs_of=(0, 1, 2), tm=tm, tn=tn, n_tiles=D_MODEL // tn, name="merge")[0]


def _residual_proj(a, w, x, mod, layer, gate_idx, name):
    tm, tn = 512, 512

    def epilogue(accs, extra, outs, j, row0):
        g = extra[1][pl.ds(_mod_row(row0), 1), :]
        outs[0][...] = extra[0][...] + g * accs[0]

    extra = [(x, pl.BlockSpec((tm, tn), lambda j, i: (i, j))),
             (mod, pl.BlockSpec((None, MOD_ROWS, tn),
                                lambda j, i: (layer, 0, gate_idx * (D_MODEL // tn) + j)))]
    return _ws_matmul([a], [(w, layer, 0)], extra,
                      [jax.ShapeDtypeStruct((T_ALL, D_MODEL), F32)],
                      [pl.BlockSpec((tm, tn), lambda j, i: (i, j))],
                      epilogue, lhs_of=(0,), tm=tm, tn=tn, n_tiles=D_MODEL // tn, name=name)[0]


def _ffn_up(h2, w_up, conv_w, conv_b, layer):
    tm, tn = DEC_SEQ, 512

    def epilogue(accs, extra, outs, j, row0):
        a, b = accs
        seq_len = jnp.where(row0 < T_CTX, SEQ, DEC_SEQ)
        pos = lax.broadcasted_iota(jnp.int32, a.shape, 0) & (seq_len - 1)
        prev = jnp.where(pos == 0, 0.0, pltpu.roll(a, 1, 0))
        nxt = jnp.where(pos == seq_len - 1, 0.0, pltpu.roll(a, tm - 1, 0))
        cw = extra[0][...]
        a = cw[0:1] * prev + cw[1:2] * a + cw[2:3] * nxt + extra[1][...]
        outs[0][...] = (_gelu(a) * b).astype(BF)

    extra = [(conv_w, pl.BlockSpec((None, 3, tn), lambda j, i: (layer, 0, j))),
             (conv_b.reshape(DEPTH, 1, D_FF), pl.BlockSpec((None, 1, tn), lambda j, i: (layer, 0, j)))]
    return _ws_matmul([h2], [(w_up, layer, 0), (w_up, layer, D_FF // tn)], extra,
                      [jax.ShapeDtypeStruct((T_ALL, D_FF), BF)],
                      [pl.BlockSpec((tm, tn), lambda j, i: (i, j))],
                      epilogue, lhs_of=(0, 0), tm=tm, tn=tn, n_tiles=D_FF // tn, name="ffn_up")[0]


def _head_norm(x, gain):
    return x * lax.rsqrt(jnp.mean(x * x, axis=-1, keepdims=True) + EPS) * gain


def _rope(x, cos, sin_signed):
    lane = lax.broadcasted_iota(jnp.int32, x.shape, 1)
    first_half = (lane & (AXIS_DIM - 1)) < AXIS_DIM // 2
    partner = jnp.where(first_half, pltpu.roll(x, HEAD_DIM - AXIS_DIM // 2, 1),
                        pltpu.roll(x, AXIS_DIM // 2, 1))
    return x * cos + partner * sin_signed


def _scores(q, k):
    return lax.dot_general(q, k, (((1,), (1,)), ((), ())), preferred_element_type=F32) * (HEAD_DIM ** -0.5)


def _attn_ctx_kernel(q_ref, k_ref, v_ref, qg_ref, kg_ref, o_ref, ko_ref):
    grp = N_HEADS // N_KV_HEADS
    for kv in range(N_KV_HEADS):
        ks = slice(kv * HEAD_DIM, (kv + 1) * HEAD_DIM)
        kn = _head_norm(k_ref[:, ks], kg_ref[...])
        ko_ref[:, ks] = kn
        kn = kn.astype(BF)
        vv = v_ref[:, ks].astype(BF)
        for g in range(grp):
            hs = slice((kv * grp + g) * HEAD_DIM, (kv * grp + g + 1) * HEAD_DIM)
            qn = _head_norm(q_ref[:, hs], qg_ref[...]).astype(BF)
            s = _scores(qn, kn)
            p = jnp.exp(s - jnp.max(s, axis=-1, keepdims=True))
            den = jnp.sum(p, axis=-1, keepdims=True)
            o = jnp.dot(p.astype(BF), vv, preferred_element_type=F32) / den
            o_ref[:, hs] = o.astype(BF)


def _attn_lat_kernel(q_ref, k_ref, v_ref, ck_ref, cv_ref, cosq_ref, sinq_ref, cosk_ref, sink_ref,
                     qg_ref, kg_ref, o_ref):
    grp = N_HEADS // N_KV_HEADS
    for kv in range(N_KV_HEADS):
        ks = slice(kv * HEAD_DIM, (kv + 1) * HEAD_DIM)
        kn = _rope(_head_norm(k_ref[:, ks], kg_ref[...]), cosk_ref[...], sink_ref[...]).astype(BF)
        vv = v_ref[:, ks].astype(BF)
        ck = ck_ref[:, ks].astype(BF)
        cv = cv_ref[:, ks].astype(BF)
        for g in range(grp):
            hs = slice((kv * grp + g) * HEAD_DIM, (kv * grp + g + 1) * HEAD_DIM)
            qn = _rope(_head_norm(q_ref[:, hs], qg_ref[...]), cosq_ref[...], sinq_ref[...]).astype(BF)
            s1 = _scores(qn, ck)
            s2 = _scores(qn, kn)
            m = jnp.maximum(jnp.max(s1, axis=-1, keepdims=True), jnp.max(s2, axis=-1, keepdims=True))
            p1 = jnp.exp(s1 - m)
            p2 = jnp.exp(s2 - m)
            den = jnp.sum(p1, axis=-1, keepdims=True) + jnp.sum(p2, axis=-1, keepdims=True)
            o = (jnp.dot(p1.astype(BF), cv, preferred_element_type=F32)
                 + jnp.dot(p2.astype(BF), vv, preferred_element_type=F32)) / den
            o_ref[:, hs] = o.astype(BF)


def _attention(z, q_gain, k_gain, cache_k, cache_v, cos_t, sin_t, layer):
    gains = [pl.BlockSpec((None, 1, HEAD_DIM), lambda *_: (layer, 0, 0))] * 2
    qg = q_gain.reshape(DEPTH, 1, HEAD_DIM)
    kg = k_gain.reshape(DEPTH, 1, HEAD_DIM)
    kcol, vcol = OFF_K // D_KV, OFF_V // D_KV
    attn_ctx, k_ctx = pl.pallas_call(
        _attn_ctx_kernel,
        grid=(BATCH,),
        in_specs=[pl.BlockSpec((SEQ, D_ATTN), lambda b: (b, 0)),
                  pl.BlockSpec((SEQ, D_KV), lambda b: (b, kcol)),
                  pl.BlockSpec((SEQ, D_KV), lambda b: (b, vcol))] + gains,
        out_specs=[pl.BlockSpec((SEQ, D_ATTN), lambda b: (b, 0)),
                   pl.BlockSpec((SEQ, D_KV), lambda b: (b, 0))],
        out_shape=[jax.ShapeDtypeStruct((T_CTX, D_ATTN), BF),
                   jax.ShapeDtypeStruct((T_CTX, D_KV), F32)],
        compiler_params=_params(1),
        name="attn_ctx",
    )(z, z, z, qg, kg)

    tq = 512
    nq = DEC_SEQ // tq
    lat0 = T_CTX // DEC_SEQ
    attn_lat = pl.pallas_call(
        _attn_lat_kernel,
        grid=(DEC_BATCH, nq),
        in_specs=[pl.BlockSpec((tq, D_ATTN), lambda b, i: ((lat0 + b) * nq + i, 0)),
                  pl.BlockSpec((DEC_SEQ, D_KV), lambda b, i: (lat0 + b, kcol)),
                  pl.BlockSpec((DEC_SEQ, D_KV), lambda b, i: (lat0 + b, vcol)),
                  pl.BlockSpec((None, PAST_LEN, D_KV), lambda b, i: (b, 0, 0)),
                  pl.BlockSpec((None, PAST_LEN, D_KV), lambda b, i: (b, 0, 0)),
                  pl.BlockSpec((tq, HEAD_DIM), lambda b, i: (i, 0)),
                  pl.BlockSpec((tq, HEAD_DIM), lambda b, i: (i, 0)),
                  pl.BlockSpec((DEC_SEQ, HEAD_DIM), lambda b, i: (0, 0)),
                  pl.BlockSpec((DEC_SEQ, HEAD_DIM), lambda b, i: (0, 0))] + gains,
        out_specs=pl.BlockSpec((tq, D_ATTN), lambda b, i: (b * nq + i, 0)),
        out_shape=jax.ShapeDtypeStruct((T_LAT, D_ATTN), BF),
        compiler_params=_params(2),
        name="attn_lat",
    )(z, z, z, cache_k, cache_v, cos_t, sin_t, cos_t, sin_t, qg, kg)
    return jnp.concatenate([attn_ctx, attn_lat], axis=0), k_ctx


def _lru_kernel(xr_ref, xg_ref, h0_ref, cw_ref, cb_ref, gw_ref, gb_ref, lam_ref, o_ref, hf_ref,
                a_s, b_s, hfw_s, hbw_s, *, seq, ct):
    n_blocks = ct // RNN_BLOCK
    n_tiles = seq // SUBLANES
    row = lax.broadcasted_iota(jnp.int32, (seq, ct), 0)
    xr = xr_ref[...]

    def shifted(x, k):
        if k > 0:
            return jnp.where(row >= k, pltpu.roll(x, k, 0), 0.0)
        return jnp.where(row < seq + k, pltpu.roll(x, seq + k, 0), 0.0)

    cw = cw_ref[...]
    xc = (cb_ref[...] + cw[0:1] * shifted(xr, 2) + cw[1:2] * shifted(xr, 1) + cw[2:3] * xr
          + cw[3:4] * shifted(xr, -1))

    for kb in range(n_blocks):
        cs = slice(kb * RNN_BLOCK, (kb + 1) * RNN_BLOCK)
        xcb = xc[:, cs]
        gates = jnp.dot(xcb.astype(BF), gw_ref[kb], preferred_element_type=F32) + gb_ref[kb]
        for d in range(2):
            base = d * 2 * RNN_BLOCK
            r = jax.nn.sigmoid(gates[:, base:base + RNN_BLOCK])
            ig = jax.nn.sigmoid(gates[:, base + RNN_BLOCK:base + 2 * RNN_BLOCK])
            lam = lam_ref[kb, d:d + 1, :]
            log_a = -LRU_C * r * jnp.log1p(jnp.exp(-lam))
            a = jnp.exp(log_a)
            a_s[d, :, cs] = a
            b_s[d, :, cs] = jnp.sqrt(1.0 - a * a) * (ig * xcb)

    sub = lax.broadcasted_iota(jnp.int32, (SUBLANES, ct), 0)

    def tile_scan(a, b, forward):
        for k in (1, 2, 4):
            if forward:
                keep = sub >= k
                shift = k
            else:
                keep = sub < SUBLANES - k
                shift = SUBLANES - k
            a_n = jnp.where(keep, pltpu.roll(a, shift, 0), 1.0)
            b_n = jnp.where(keep, pltpu.roll(b, shift, 0), 0.0)
            b = a * b_n + b
            a = a * a_n
        return a, b

    def body(j, carry):
        hf, hb = carry
        r0 = pl.multiple_of(j * SUBLANES, SUBLANES)
        a, b = tile_scan(a_s[0, pl.ds(r0, SUBLANES), :], b_s[0, pl.ds(r0, SUBLANES), :], True)
        h = a * hf + b
        hfw_s[pl.ds(r0, SUBLANES), :] = h
        hf = h[SUBLANES - 1:SUBLANES, :]
        r1 = pl.multiple_of((n_tiles - 1 - j) * SUBLANES, SUBLANES)
        a, b = tile_scan(a_s[1, pl.ds(r1, SUBLANES), :], b_s[1, pl.ds(r1, SUBLANES), :], False)
        h = a * hb + b
        hbw_s[pl.ds(r1, SUBLANES), :] = h
        hb = h[0:1, :]
        return hf, hb

    hf, hb = lax.fori_loop(0, n_tiles, body, (h0_ref[0:1, :], h0_ref[1:2, :]))
    hf_ref[0:1, :] = hf
    hf_ref[1:2, :] = hb
    o_ref[...] = (_gelu(xg_ref[...]) * (hfw_s[...] + hbw_s[...])).astype(BF)


def _lru_call(z, h0, cw, cb, gw, gb, lam, layer, *, n_seq, seq, tok0, ct):
    nc = D_RNN // ct
    s0 = tok0 // seq
    blk = ct // RNN_BLOCK
    return pl.pallas_call(
        functools.partial(_lru_kernel, seq=seq, ct=ct),
        grid=(n_seq, nc),
        in_specs=[pl.BlockSpec((seq, ct), lambda s, c: (s0 + s, OFF_XR // ct + c)),
                  pl.BlockSpec((seq, ct), lambda s, c: (s0 + s, OFF_XG // ct + c)),
                  pl.BlockSpec((None, 2, ct), lambda s, c: (s, 0, c)),
                  pl.BlockSpec((None, RNN_CONV_W, ct), lambda s, c: (layer, 0, c)),
                  pl.BlockSpec((None, 1, ct), lambda s, c: (layer, 0, c)),
                  pl.BlockSpec((None, blk, RNN_BLOCK, 4 * RNN_BLOCK), lambda s, c: (layer, c, 0, 0)),
                  pl.BlockSpec((None, blk, 1, 4 * RNN_BLOCK), lambda s, c: (layer, c, 0, 0)),
                  pl.BlockSpec((None, blk, 2, RNN_BLOCK), lambda s, c: (layer, c, 0, 0))],
        out_specs=[pl.BlockSpec((seq, ct), lambda s, c: (s, c)),
                   pl.BlockSpec((None, 2, ct), lambda s, c: (s, 0, c))],
        out_shape=[jax.ShapeDtypeStruct((n_seq * seq, D_RNN), BF),
                   jax.ShapeDtypeStruct((n_seq, 2, D_RNN), F32)],
        scratch_shapes=[pltpu.VMEM((2, seq, ct), F32), pltpu.VMEM((2, seq, ct), F32),
                        pltpu.VMEM((seq, ct), F32), pltpu.VMEM((seq, ct), F32)],
        compiler_params=_params(2),
        name="rg_lru",
    )(z, z, h0, cw, cb, gw, gb, lam)


def _s5_disc_kernel(are_ref, aim_ref, ldt_ref, bre_ref, bim_ref, abr_ref, abi_ref, bbr_ref, bbi_ref):
    lam_r = jnp.minimum(are_ref[...], -1e-4)
    lam_i = aim_ref[...]
    dt = jnp.exp(ldt_ref[...])
    mag = jnp.exp(lam_r * dt)
    ab_r = mag * jnp.cos(lam_i * dt)
    ab_i = mag * jnp.sin(lam_i * dt)
    den = lam_r * lam_r + lam_i * lam_i
    q_r = ((ab_r - 1.0) * lam_r + ab_i * lam_i) / den
    q_i = (ab_i * lam_r - (ab_r - 1.0) * lam_i) / den
    abr_ref[...] = ab_r
    abi_ref[...] = ab_i
    br = bre_ref[...]
    bi = bim_ref[...]
    bbr_ref[...] = q_r[:, None, :] * br - q_i[:, None, :] * bi
    bbi_ref[...] = q_r[:, None, :] * bi + q_i[:, None, :] * br


def _s5_discretize(a_re, a_im, log_dt, b_re, b_im):
    n = DEPTH * 2 * N_SSM_GROUPS
    bt_re = jnp.swapaxes(b_re, -1, -2).reshape(n, SSM_GROUP, SSM_STATE)
    bt_im = jnp.swapaxes(b_im, -1, -2).reshape(n, SSM_GROUP, SSM_STATE)
    vec = jax.ShapeDtypeStruct((n, SSM_STATE), F32)
    mat = jax.ShapeDtypeStruct((n, SSM_GROUP, SSM_STATE), F32)
    return pl.pallas_call(
        _s5_disc_kernel,
        out_shape=[vec, vec, mat, mat],
        compiler_params=pltpu.CompilerParams(vmem_limit_bytes=VMEM_LIMIT),
        name="s5_discretize",
    )(a_re.reshape(n, SSM_STATE), a_im.reshape(n, SSM_STATE), log_dt.reshape(n, 1), bt_re, bt_im)


def _s5_kernel(u_ref, h0r_ref, h0i_ref, ar_ref, ai_ref, wb_ref, wcr_ref, wci_ref,
               y_ref, fr_ref, fi_ref, bu_s, h_s, *, steps):
    rows = S5_CHUNK * SUBLANES
    rid = lax.broadcasted_iota(jnp.int32, (rows, LANES), 0)
    is_fwd = (rid & (SUBLANES - 1)) < S5_SEQS
    ar = ar_ref[...]
    ai = ai_ref[...]

    def chunk(c, carry):
        r0 = pl.multiple_of(c * rows, rows)
        u = u_ref[pl.ds(r0, rows), :]
        lhs = jnp.concatenate([jnp.where(is_fwd, u, 0.0), jnp.where(is_fwd, 0.0, u)], axis=1).astype(BF)
        bu_s[...] = jnp.dot(lhs, wb_ref[...], preferred_element_type=F32)

        def step(s, hc):
            hr, hi = hc
            q0 = pl.multiple_of(s * SUBLANES, SUBLANES)
            br = bu_s[pl.ds(q0, SUBLANES), 0:S5_STATES]
            bi = bu_s[pl.ds(q0, SUBLANES), S5_STATES:2 * S5_STATES]
            nr = ar * hr - ai * hi + br
            ni = ar * hi + ai * hr + bi
            h_s[pl.ds(q0, SUBLANES), 0:S5_STATES] = nr
            h_s[pl.ds(q0, SUBLANES), S5_STATES:2 * S5_STATES] = ni
            return nr, ni

        carry = lax.fori_loop(0, S5_CHUNK, step, carry, unroll=8)
        y = (jnp.dot(h_s[:, 0:S5_STATES].astype(BF), wcr_ref[...], preferred_element_type=F32)
             - jnp.dot(h_s[:, S5_STATES:2 * S5_STATES].astype(BF), wci_ref[...], preferred_element_type=F32))
        y_ref[pl.ds(r0, rows), :] = jnp.where(is_fwd, y[:, 0:LANES], y[:, LANES:2 * LANES])
        return carry

    hr, hi = lax.fori_loop(0, steps // S5_CHUNK, chunk, (h0r_ref[...], h0i_ref[...]))
    fr_ref[...] = hr
    fi_ref[...] = hi


def _s5_call(u_p, h0r, h0i, ab_r, ab_i, wb, wcr, wci, layer, *, n_blk, steps):
    rows = steps * SUBLANES
    n_st = N_SSM_GROUPS * SSM_STATE
    st_spec = pl.BlockSpec((None, SUBLANES, S5_STATES), lambda b, t: (b, 0, t))
    ab_spec = pl.BlockSpec((None, SUBLANES, S5_STATES), lambda b, t: (layer, 0, t))
    return pl.pallas_call(
        functools.partial(_s5_kernel, steps=steps),
        grid=(n_blk, S5_TILES),
        in_specs=[pl.BlockSpec((None, rows, LANES), lambda b, t: (b, 0, t)),
                  st_spec, st_spec, ab_spec, ab_spec,
                  pl.BlockSpec((None, None, 2 * LANES, 2 * S5_STATES), lambda b, t: (layer, t, 0, 0)),
                  pl.BlockSpec((None, None, S5_STATES, 2 * LANES), lambda b, t: (layer, t, 0, 0)),
                  pl.BlockSpec((None, None, S5_STATES, 2 * LANES), lambda b, t: (layer, t, 0, 0))],
        out_specs=[pl.BlockSpec((None, rows, LANES), lambda b, t: (b, 0, t)), st_spec, st_spec],
        out_shape=[jax.ShapeDtypeStruct((n_blk, rows, D_SSM), F32),
                   jax.ShapeDtypeStruct((n_blk, SUBLANES, n_st), F32),
                   jax.ShapeDtypeStruct((n_blk, SUBLANES, n_st), F32)],
        scratch_shapes=[pltpu.VMEM((S5_CHUNK * SUBLANES, 2 * S5_STATES), F32),
                        pltpu.VMEM((S5_CHUNK * SUBLANES, 2 * S5_STATES), F32)],
        compiler_params=_params(2),
        name="s5_scan",
    )(u_p, h0r, h0i, ab_r, ab_i, wb, wcr, wci)


def _s5_permute(u, n_seq, seq):
    c = u.shape[-1]
    x = u.reshape(n_seq // S5_SEQS, S5_SEQS, seq, c)
    x = jnp.stack([x, x[:, :, ::-1]], axis=1)
    return x.transpose(0, 3, 1, 2, 4).reshape(n_seq // S5_SEQS, seq * SUBLANES, c)


def _s5_unpermute(y, n_seq, seq):
    c = y.shape[-1]
    x = y.reshape(n_seq // S5_SEQS, seq, 2, S5_SEQS, c).transpose(2, 0, 3, 1, 4)
    return x[0].reshape(n_seq * seq, c), x[1][:, :, ::-1].reshape(n_seq * seq, c)


def _s5_state_rows(st):
    n = st.shape[0]
    x = st.reshape(n // S5_SEQS, S5_SEQS, 2, N_SSM_GROUPS * SSM_STATE)
    return x.transpose(0, 2, 1, 3).reshape(n // S5_SEQS, SUBLANES, N_SSM_GROUPS * SSM_STATE)


def _s5_state_unrows(st, n_seq):
    x = st.reshape(n_seq // S5_SEQS, 2, S5_SEQS, N_SSM_GROUPS, SSM_STATE)
    return x.transpose(0, 2, 1, 3, 4).reshape(n_seq, 2, N_SSM_GROUPS, SSM_STATE)


def _glu_kernel(yf_ref, yb_ref, u_ref, d_ref, w_ref, b_ref, o_ref, wbf):
    @pl.when(pl.program_id(0) == 0)
    def _():
        wbf[...] = w_ref[...].astype(BF)

    ssm = _gelu(yf_ref[...] + yb_ref[...] + d_ref[...] * u_ref[...])
    gate = jnp.dot(ssm.astype(BF), wbf[...], preferred_element_type=F32) + b_ref[...]
    o_ref[...] = (ssm * jax.nn.sigmoid(gate)).astype(BF)


def _s5_glu(yf, yb, us, d_skip, glu_w, glu_b, layer):
    tm = 512
    tok = pl.BlockSpec((tm, D_SSM), lambda i: (i, 0))
    vec = pl.BlockSpec((None, 1, D_SSM), lambda i: (layer, 0, 0))
    return pl.pallas_call(
        _glu_kernel,
        grid=(T_ALL // tm,),
        in_specs=[tok, tok, tok, vec,
                  pl.BlockSpec((None, D_SSM, D_SSM), lambda i: (layer, 0, 0)), vec],
        out_specs=tok,
        out_shape=jax.ShapeDtypeStruct((T_ALL, D_SSM), BF),
        scratch_shapes=[pltpu.VMEM((D_SSM, D_SSM), BF)],
        compiler_params=_params(1),
        name="s5_glu",
    )(yf, yb, us, d_skip.reshape(DEPTH, 1, D_SSM), glu_w, glu_b.reshape(DEPTH, 1, D_SSM))


def _rope_tables():
    rows = DEC_SEQ // GRID_W
    t_row = jnp.repeat(jnp.arange(rows, dtype=F32), GRID_W)
    t_col = (jnp.arange(DEC_SEQ) % GRID_W).astype(F32)
    inv = 1.0 / (ROPE_THETA ** (jnp.arange(0, AXIS_DIM, 2, dtype=F32) / AXIS_DIM))
    ang_r = t_row[:, None] * inv
    ang_c = t_col[:, None] * inv
    cos = jnp.concatenate([jnp.cos(ang_r), jnp.cos(ang_r), jnp.cos(ang_c), jnp.cos(ang_c)], axis=-1)
    sin = jnp.concatenate([-jnp.sin(ang_r), jnp.sin(ang_r), -jnp.sin(ang_c), jnp.sin(ang_c)], axis=-1)
    return cos, sin


def _block_diag(blocks):
    n, r, c = blocks.shape[-3:]
    eye = jnp.eye(n, dtype=blocks.dtype)
    out = blocks[..., :, :, None, :] * eye[:, None, :, None]
    return out.reshape(blocks.shape[:-3] + (n * r, n * c))


def kernel(x_prompt, x_sample, c, cache_k, cache_v, state_lru, state_s5_re, state_s5_im, c_ctx, w_mod, b_mod, g_mix, w_in, q_gain, k_gain, rnn_conv_w, rnn_conv_b, lru_gate_w, lru_gate_b, lru_lambda, s5_a_re, s5_a_im, s5_log_dt, s5_b_re, s5_b_im, s5_c_re, s5_c_im, s5_d, glu_w, glu_b, w_pa, w_pr, w_ps, w_o, g_ffn, w_up, ffn_conv_w, ffn_conv_b, w_down):
    x = jnp.concatenate([x_prompt.reshape(T_CTX, D_MODEL), x_sample.reshape(T_LAT, D_MODEL)], axis=0)

    cvec = jnp.zeros((MOD_ROWS, D_MODEL), F32).at[0].set(c_ctx).at[1:1 + DEC_BATCH].set(c)
    mod = _modulation(cvec, w_mod, b_mod)
    cos_t, sin_t = _rope_tables()

    gw = lru_gate_w.transpose(0, 3, 4, 1, 2, 5).reshape(DEPTH, N_RNN_BLOCKS, RNN_BLOCK, 4 * RNN_BLOCK).astype(BF)
    gb = (lru_gate_b.reshape(DEPTH, 2, 2, N_RNN_BLOCKS, RNN_BLOCK).transpose(0, 3, 1, 2, 4)
          .reshape(DEPTH, N_RNN_BLOCKS, 1, 4 * RNN_BLOCK))
    lam = lru_lambda.reshape(DEPTH, 2, N_RNN_BLOCKS, RNN_BLOCK).transpose(0, 2, 1, 3)
    cb = rnn_conv_b.reshape(DEPTH, 1, D_RNN)

    ab_r, ab_i, bbt_r, bbt_i = _s5_discretize(s5_a_re, s5_a_im, s5_log_dt, s5_b_re, s5_b_im)
    n_st = N_SSM_GROUPS * SSM_STATE
    ab_r = jnp.repeat(ab_r.reshape(DEPTH, 2, n_st), S5_SEQS, axis=1)
    ab_i = jnp.repeat(ab_i.reshape(DEPTH, 2, n_st), S5_SEQS, axis=1)
    bshape = (DEPTH, 2, S5_TILES, S5_GROUPS, SSM_GROUP, SSM_STATE)
    wb = jnp.concatenate([_block_diag(bbt_r.reshape(bshape)), _block_diag(bbt_i.reshape(bshape))], axis=-1)
    wb = wb.transpose(0, 2, 1, 3, 4).reshape(DEPTH, S5_TILES, 2 * LANES, 2 * S5_STATES).astype(BF)
    cshape = (DEPTH, 2, S5_TILES, S5_GROUPS, SSM_GROUP, SSM_STATE)
    wcr = _block_diag(jnp.swapaxes(s5_c_re.reshape(cshape), -1, -2))
    wci = _block_diag(jnp.swapaxes(s5_c_im.reshape(cshape), -1, -2))
    wcr = wcr.transpose(0, 2, 3, 1, 4).reshape(DEPTH, S5_TILES, S5_STATES, 2 * LANES).astype(BF)
    wci = wci.transpose(0, 2, 3, 1, 4).reshape(DEPTH, S5_TILES, S5_STATES, 2 * LANES).astype(BF)

    zeros_lru = jnp.zeros((BATCH, 2, D_RNN), F32)
    zeros_s5 = jnp.zeros((BATCH // S5_SEQS, SUBLANES, n_st), F32)

    new_k, new_v, new_lru, new_sr, new_si = [], [], [], [], []
    for l in range(DEPTH):
        h = _norm_mod(x, g_mix, mod, l, 0, 1)
        z = _in_proj(h, w_in, l)

        ck = cache_k[:, l].reshape(DEC_BATCH, PAST_LEN, D_KV)
        cv = cache_v[:, l].reshape(DEC_BATCH, PAST_LEN, D_KV)
        attn, k_ctx = _attention(z, q_gain, k_gain, ck, cv, cos_t, sin_t, l)
        new_k.append(k_ctx.reshape(BATCH, SEQ, N_KV_HEADS, HEAD_DIM))
        new_v.append(z[:T_CTX, OFF_V:OFF_V + D_KV].reshape(BATCH, SEQ, N_KV_HEADS, HEAD_DIM))

        rec_c, lru_f = _lru_call(z, zeros_lru, rnn_conv_w, cb, gw, gb, lam, l,
                                 n_seq=BATCH, seq=SEQ, tok0=0, ct=512)
        rec_l, _ = _lru_call(z, state_lru[:, l], rnn_conv_w, cb, gw, gb, lam, l,
                             n_seq=DEC_BATCH, seq=DEC_SEQ, tok0=T_CTX, ct=256)
        rec = jnp.concatenate([rec_c, rec_l], axis=0)
        new_lru.append(lru_f)

        us = z[:, OFF_US:OFF_US + D_SSM]
        y_c, f_r, f_i = _s5_call(_s5_permute(us[:T_CTX], BATCH, SEQ), zeros_s5, zeros_s5,
                                 ab_r, ab_i, wb, wcr, wci, l, n_blk=BATCH // S5_SEQS, steps=SEQ)
        y_l, _, _ = _s5_call(_s5_permute(us[T_CTX:], DEC_BATCH, DEC_SEQ),
                             _s5_state_rows(state_s5_re[:, l]), _s5_state_rows(state_s5_im[:, l]),
                             ab_r, ab_i, wb, wcr, wci, l, n_blk=DEC_BATCH // S5_SEQS, steps=DEC_SEQ)
        yf_c, yb_c = _s5_unpermute(y_c, BATCH, SEQ)
        yf_l, yb_l = _s5_unpermute(y_l, DEC_BATCH, DEC_SEQ)
        ssm = _s5_glu(jnp.concatenate([yf_c, yf_l], axis=0), jnp.concatenate([yb_c, yb_l], axis=0),
                      us, s5_d, glu_w, glu_b, l)
        new_sr.append(_s5_state_unrows(f_r, BATCH))
        new_si.append(_s5_state_unrows(f_i, BATCH))

        merged = _merge(attn, rec, ssm, z, w_pa, w_pr, w_ps, l)
        x = _residual_proj(merged, w_o, x, mod, l, 2, "out_proj")

        h2 = _norm_mod(x, g_ffn, mod, l, 3, 4)
        hmid = _ffn_up(h2, w_up, ffn_conv_w, ffn_conv_b, l)
        x = _residual_proj(hmid, w_down, x, mod, l, 5, "ffn_down")

    return (x[:T_CTX].reshape(BATCH, SEQ, D_MODEL), x[T_CTX:].reshape(DEC_BATCH, DEC_SEQ, D_MODEL),
            jnp.stack(new_k, axis=1), jnp.stack(new_v, axis=1), jnp.stack(new_lru, axis=1),
            jnp.stack(new_sr, axis=1), jnp.stack(new_si, axis=1))
```

```python
import functools
import math

import jax
import jax.numpy as jnp
from jax import lax
from jax.experimental import pallas as pl
from jax.experimental.pallas import tpu as pltpu

F32 = jnp.float32
BF = jnp.bfloat16

D_MODEL = 2048
BATCH = 32
SEQ = 256
DEPTH = 4
DEC_BATCH = 4
DEC_SEQ = 1024
PAST_LEN = 256
GRID_W = 64
N_HEADS = 8
N_KV_HEADS = 2
HEAD_DIM = 128
AXIS_DIM = HEAD_DIM // 2
ROPE_THETA = 10000.0
D_ATTN = N_HEADS * HEAD_DIM
D_KV = N_KV_HEADS * HEAD_DIM
D_RNN = D_MODEL // 2
RNN_BLOCK = 128
N_RNN_BLOCKS = D_RNN // RNN_BLOCK
RNN_CONV_W = 4
LRU_C = 8.0
D_SSM = D_MODEL // 2
SSM_GROUP = 16
N_SSM_GROUPS = D_SSM // SSM_GROUP
SSM_STATE = 64
D_FF = 2 * D_MODEL
N_MOD = 6
EPS = 1e-6
D_IN = D_ATTN + 2 * D_KV + 2 * D_RNN + D_SSM + 3 * D_MODEL

OFF_Q = 0
OFF_K = D_ATTN
OFF_V = OFF_K + D_KV
OFF_XR = OFF_V + D_KV
OFF_XG = OFF_XR + D_RNN
OFF_US = OFF_XG + D_RNN
OFF_GL = OFF_US + D_SSM

T_CTX = BATCH * SEQ
T_LAT = DEC_BATCH * DEC_SEQ
T_ALL = T_CTX + T_LAT
MOD_ROWS = 8

SUBLANES = 8
LANES = 128
VMEM_LIMIT = 56 * 1024 * 1024

S5_SEQS = SUBLANES // 2
S5_GROUPS = LANES // SSM_GROUP
S5_TILES = N_SSM_GROUPS // S5_GROUPS
S5_PAIRS = S5_TILES // 2
S5_STATES = S5_GROUPS * SSM_STATE
S5_CHUNK = 128
S5_LAT_ROW0 = (BATCH // S5_SEQS) * SEQ
S5_ROWS = S5_LAT_ROW0 + DEC_SEQ


def _params(n_axes):
    return pltpu.CompilerParams(dimension_semantics=("arbitrary",) * n_axes,
                                vmem_limit_bytes=VMEM_LIMIT)


def _mod_row(first_token):
    return jnp.where(first_token < T_CTX, 0, 1 + jnp.maximum(first_token - T_CTX, 0) // DEC_SEQ)


def _gelu(x):
    return jax.nn.gelu(x)


def _mod_kernel(c_ref, w_ref, b_ref, o_ref):
    c = c_ref[...]
    s = (c * jax.nn.sigmoid(c)).astype(BF)
    o_ref[...] = jnp.dot(s, w_ref[...].astype(BF), preferred_element_type=F32) + b_ref[...]


def _modulation(cvec, w_mod, b_mod):
    tn = 1536
    n_out = N_MOD * D_MODEL
    return pl.pallas_call(
        _mod_kernel,
        grid=(DEPTH, n_out // tn),
        in_specs=[pl.BlockSpec((MOD_ROWS, D_MODEL), lambda l, j: (0, 0)),
                  pl.BlockSpec((None, D_MODEL, tn), lambda l, j: (l, 0, j)),
                  pl.BlockSpec((None, 1, tn), lambda l, j: (l, 0, j))],
        out_specs=pl.BlockSpec((None, MOD_ROWS, tn), lambda l, j: (l, 0, j)),
        out_shape=jax.ShapeDtypeStruct((DEPTH, MOD_ROWS, n_out), F32),
        compiler_params=_params(2),
        name="modulation",
    )(cvec, w_mod, b_mod.reshape(DEPTH, 1, n_out))


def _norm_kernel(x_ref, g_ref, sc_ref, sh_ref, o_ref, *, tm):
    r = _mod_row(pl.program_id(0) * tm)
    x = x_ref[...]
    y = x * lax.rsqrt(jnp.mean(x * x, axis=-1, keepdims=True) + EPS)
    y = y * g_ref[...]
    sc = sc_ref[pl.ds(r, 1), :]
    sh = sh_ref[pl.ds(r, 1), :]
    o_ref[...] = (y * (1.0 + sc) + sh).astype(BF)


def _norm_mod(x, gain, mod, layer, shift_idx, scale_idx):
    tm = 512
    return pl.pallas_call(
        functools.partial(_norm_kernel, tm=tm),
        grid=(T_ALL // tm,),
        in_specs=[pl.BlockSpec((tm, D_MODEL), lambda i: (i, 0)),
                  pl.BlockSpec((None, 1, D_MODEL), lambda i: (layer, 0, 0)),
                  pl.BlockSpec((None, MOD_ROWS, D_MODEL), lambda i: (layer, 0, scale_idx)),
                  pl.BlockSpec((None, MOD_ROWS, D_MODEL), lambda i: (layer, 0, shift_idx))],
        out_specs=pl.BlockSpec((tm, D_MODEL), lambda i: (i, 0)),
        out_shape=jax.ShapeDtypeStruct((T_ALL, D_MODEL), BF),
        compiler_params=_params(1),
        name="norm_mod",
    )(x, gain.reshape(DEPTH, 1, D_MODEL), mod, mod)


def _ws_kernel(*refs, n_a, n_w, n_extra, n_out, lhs_of, epilogue, tm):
    a_refs = refs[:n_a]
    w_refs = refs[n_a:n_a + n_w]
    extra = refs[n_a + n_w:n_a + n_w + n_extra]
    outs = refs[n_a + n_w + n_extra:n_a + n_w + n_extra + n_out]
    wbf = refs[n_a + n_w + n_extra + n_out:]
    j = pl.program_id(0)
    i = pl.program_id(1)

    @pl.when(i == 0)
    def _():
        for k in range(n_w):
            wbf[k][...] = w_refs[k][...].astype(BF)

    accs = [jnp.dot(a_refs[lhs_of[k]][...], wbf[k][...], preferred_element_type=F32)
            for k in range(n_w)]
    epilogue(accs, extra, outs, j, i * tm)


def _ws_matmul(a_list, w_list, extra, out_shapes, out_specs, epilogue, *, lhs_of, tm, tn, n_tiles, name):
    in_specs, args, scratch = [], [], []
    for a in a_list:
        in_specs.append(pl.BlockSpec((tm, a.shape[1]), lambda j, i: (i, 0)))
        args.append(a)
    for w, layer, off in w_list:
        kdim = w.shape[1]
        in_specs.append(pl.BlockSpec((None, kdim, tn), lambda j, i, layer=layer, off=off: (layer, 0, off + j)))
        args.append(w)
        scratch.append(pltpu.VMEM((kdim, tn), BF))
    for arr, spec in extra:
        in_specs.append(spec)
        args.append(arr)
    kern = functools.partial(_ws_kernel, n_a=len(a_list), n_w=len(w_list), n_extra=len(extra),
                             n_out=len(out_shapes), lhs_of=lhs_of, epilogue=epilogue, tm=tm)
    return pl.pallas_call(
        kern,
        grid=(n_tiles, T_ALL // tm),
        in_specs=in_specs,
        out_specs=out_specs,
        out_shape=out_shapes,
        scratch_shapes=scratch,
        compiler_params=_params(2),
        name=name,
    )(*args)


def _segment_block(i):
    il = i - BATCH
    per_lat = DEC_SEQ // SEQ
    slot = jnp.where(i < BATCH, i % S5_SEQS, il // per_lat)
    rblk = jnp.where(i < BATCH, i // S5_SEQS, S5_LAT_ROW0 // SEQ + il % per_lat)
    return slot, rblk


def _store_acc(accs, extra, outs, j, row0):
    outs[0][...] = accs[0]


def _in_proj(h, w_in, layer):
    tm = 512
    tn_a = 896
    z_a = _ws_matmul([h], [(w_in, layer, 0)], [],
                     [jax.ShapeDtypeStruct((T_ALL, OFF_US), F32)],
                     [pl.BlockSpec((tm, tn_a), lambda j, i: (i, j))],
                     _store_acc, lhs_of=(0,), tm=tm, tn=tn_a, n_tiles=OFF_US // tn_a, name="in_proj_a")[0]
    tn_g = 1536
    z_g = _ws_matmul([h], [(w_in, layer, OFF_GL // tn_g)], [],
                     [jax.ShapeDtypeStruct((T_ALL, 3 * D_MODEL), F32)],
                     [pl.BlockSpec((tm, tn_g), lambda j, i: (i, j))],
                     _store_acc, lhs_of=(0,), tm=tm, tn=tn_g, n_tiles=3 * D_MODEL // tn_g, name="in_proj_g")[0]

    tn_u = 512

    def us_index(j, i):
        slot, rblk = _segment_block(i)
        return slot, rblk, j

    us = _ws_matmul([h], [(w_in, layer, OFF_US // tn_u)], [],
                    [jax.ShapeDtypeStruct((S5_SEQS, S5_ROWS, D_SSM), F32)],
                    [pl.BlockSpec((None, SEQ, tn_u), us_index)],
                    _store_acc, lhs_of=(0,), tm=SEQ, tn=tn_u, n_tiles=D_SSM // tn_u, name="in_proj_u")[0]
    return z_a, z_g, us


def _merge(attn, rec, ssm, z, w_pa, w_pr, w_ps, layer):
    tm, tn = 512, 512
    gl0 = 0

    def epilogue(accs, extra, outs, j, row0):
        m = jax.nn.sigmoid(extra[0][...]) * accs[0]
        m = m + jax.nn.sigmoid(extra[1][...]) * accs[1]
        m = m + jax.nn.sigmoid(extra[2][...]) * accs[2]
        outs[0][...] = m.astype(BF)

    gates = [(z, pl.BlockSpec((tm, tn), lambda j, i, b=b: (i, gl0 + b * (D_MODEL // tn) + j)))
             for b in range(3)]
    return _ws_matmul([attn, rec, ssm], [(w_pa, layer, 0), (w_pr, layer, 0), (w_ps, layer, 0)], gates,
                      [jax.ShapeDtypeStruct((T_ALL, D_MODEL), BF)],
                      [pl.BlockSpec((tm, tn), lambda j, i: (i, j))],
                      epilogue, lhs_of=(0, 1, 2), tm=tm, tn=tn, n_tiles=D_MODEL // tn, name="merge")[0]


def _residual_proj(a, w, x, mod, layer, gate_idx, name):
    tm, tn = 512, 512

    def epilogue(accs, extra, outs, j, row0):
        g = extra[1][pl.ds(_mod_row(row0), 1), :]
        outs[0][...] = extra[0][...] + g * accs[0]

    extra = [(x, pl.BlockSpec((tm, tn), lambda j, i: (i, j))),
             (mod, pl.BlockSpec((None, MOD_ROWS, tn),
                                lambda j, i: (layer, 0, gate_idx * (D_MODEL // tn) + j)))]
    return _ws_matmul([a], [(w, layer, 0)], extra,
                      [jax.ShapeDtypeStruct((T_ALL, D_MODEL), F32)],
                      [pl.BlockSpec((tm, tn), lambda j, i: (i, j))],
                      epilogue, lhs_of=(0,), tm=tm, tn=tn, n_tiles=D_MODEL // tn, name=name)[0]


def _ffn_up(h2, w_up, conv_w, conv_b, layer):
    tm, tn = DEC_SEQ, 512

    def epilogue(accs, extra, outs, j, row0):
        a, b = accs
        seq_len = jnp.where(row0 < T_CTX, SEQ, DEC_SEQ)
        pos = lax.broadcasted_iota(jnp.int32, a.shape, 0) & (seq_len - 1)
        prev = jnp.where(pos == 0, 0.0, pltpu.roll(a, 1, 0))
        nxt = jnp.where(pos == seq_len - 1, 0.0, pltpu.roll(a, tm - 1, 0))
        cw = extra[0][...]
        a = cw[0:1] * prev + cw[1:2] * a + cw[2:3] * nxt + extra[1][...]
        outs[0][...] = (_gelu(a) * b).astype(BF)

    extra = [(conv_w, pl.BlockSpec((None, 3, tn), lambda j, i: (layer, 0, j))),
             (conv_b.reshape(DEPTH, 1, D_FF), pl.BlockSpec((None, 1, tn), lambda j, i: (layer, 0, j)))]
    return _ws_matmul([h2], [(w_up, layer, 0), (w_up, layer, D_FF // tn)], extra,
                      [jax.ShapeDtypeStruct((T_ALL, D_FF), BF)],
                      [pl.BlockSpec((tm, tn), lambda j, i: (i, j))],
                      epilogue, lhs_of=(0, 0), tm=tm, tn=tn, n_tiles=D_FF // tn, name="ffn_up")[0]


def _head_norm(x, gain):
    return x * lax.rsqrt(jnp.mean(x * x, axis=-1, keepdims=True) + EPS) * gain


def _rope(x, cos, sin_signed):
    lane = lax.broadcasted_iota(jnp.int32, x.shape, 1)
    first_half = (lane & (AXIS_DIM - 1)) < AXIS_DIM // 2
    partner = jnp.where(first_half, pltpu.roll(x, HEAD_DIM - AXIS_DIM // 2, 1),
                        pltpu.roll(x, AXIS_DIM // 2, 1))
    return x * cos + partner * sin_signed


def _scores(q, k):
    return lax.dot_general(q, k, (((1,), (1,)), ((), ())), preferred_element_type=F32) * (HEAD_DIM ** -0.5)


def _attn_ctx_kernel(q_ref, k_ref, v_ref, qg_ref, kg_ref, o_ref, ko_ref):
    grp = N_HEADS // N_KV_HEADS
    for kv in range(N_KV_HEADS):
        ks = slice(kv * HEAD_DIM, (kv + 1) * HEAD_DIM)
        kn = _head_norm(k_ref[:, ks], kg_ref[...])
        ko_ref[:, ks] = kn
        kn = kn.astype(BF)
        vv = v_ref[:, ks].astype(BF)
        for g in range(grp):
            hs = slice((kv * grp + g) * HEAD_DIM, (kv * grp + g + 1) * HEAD_DIM)
            qn = _head_norm(q_ref[:, hs], qg_ref[...]).astype(BF)
            s = _scores(qn, kn)
            p = jnp.exp(s - jnp.max(s, axis=-1, keepdims=True))
            den = jnp.sum(p, axis=-1, keepdims=True)
            o = jnp.dot(p.astype(BF), vv, preferred_element_type=F32) / den
            o_ref[:, hs] = o.astype(BF)


def _attn_lat_kernel(q_ref, k_ref, v_ref, ck_ref, cv_ref, cosq_ref, sinq_ref, cosk_ref, sink_ref,
                     qg_ref, kg_ref, filled_ref, o_ref):
    del filled_ref
    grp = N_HEADS // N_KV_HEADS
    for kv in range(N_KV_HEADS):
        ks = slice(kv * HEAD_DIM, (kv + 1) * HEAD_DIM)
        kn = _rope(_head_norm(k_ref[:, ks], kg_ref[...]), cosk_ref[...], sink_ref[...]).astype(BF)
        vv = v_ref[:, ks].astype(BF)
        ck = ck_ref[:, ks].astype(BF)
        cv = cv_ref[:, ks].astype(BF)
        for g in range(grp):
            hs = slice((kv * grp + g) * HEAD_DIM, (kv * grp + g + 1) * HEAD_DIM)
            qn = _rope(_head_norm(q_ref[:, hs], qg_ref[...]), cosq_ref[...], sinq_ref[...]).astype(BF)
            s1 = _scores(qn, ck)
            s2 = _scores(qn, kn)
            m = jnp.maximum(jnp.max(s1, axis=-1, keepdims=True), jnp.max(s2, axis=-1, keepdims=True))
            p1 = jnp.exp(s1 - m)
            p2 = jnp.exp(s2 - m)
            den = jnp.sum(p1, axis=-1, keepdims=True) + jnp.sum(p2, axis=-1, keepdims=True)
            o = (jnp.dot(p1.astype(BF), cv, preferred_element_type=F32)
                 + jnp.dot(p2.astype(BF), vv, preferred_element_type=F32)) / den
            o_ref[:, hs] = o.astype(BF)


def _attention(z, q_gain, k_gain, cache_k, cache_v, cos_t, sin_t, layer):
    gains = [pl.BlockSpec((None, 1, HEAD_DIM), lambda *_: (layer, 0, 0))] * 2
    qg = q_gain.reshape(DEPTH, 1, HEAD_DIM)
    kg = k_gain.reshape(DEPTH, 1, HEAD_DIM)
    kcol, vcol = OFF_K // D_KV, OFF_V // D_KV
    attn, k_ctx = pl.pallas_call(
        _attn_ctx_kernel,
        grid=(BATCH,),
        in_specs=[pl.BlockSpec((SEQ, D_ATTN), lambda b: (b, 0)),
                  pl.BlockSpec((SEQ, D_KV), lambda b: (b, kcol)),
                  pl.BlockSpec((SEQ, D_KV), lambda b: (b, vcol))] + gains,
        out_specs=[pl.BlockSpec((SEQ, D_ATTN), lambda b: (b, 0)),
                   pl.BlockSpec((SEQ, D_KV), lambda b: (b, 0))],
        out_shape=[jax.ShapeDtypeStruct((T_ALL, D_ATTN), BF),
                   jax.ShapeDtypeStruct((T_CTX, D_KV), F32)],
        compiler_params=_params(1),
        name="attn_ctx",
    )(z, z, z, qg, kg)

    tq = 512
    nq = DEC_SEQ // tq
    lat0 = T_CTX // DEC_SEQ
    attn = pl.pallas_call(
        _attn_lat_kernel,
        grid=(DEC_BATCH, nq),
        in_specs=[pl.BlockSpec((tq, D_ATTN), lambda b, i: ((lat0 + b) * nq + i, 0)),
                  pl.BlockSpec((DEC_SEQ, D_KV), lambda b, i: (lat0 + b, kcol)),
                  pl.BlockSpec((DEC_SEQ, D_KV), lambda b, i: (lat0 + b, vcol)),
                  pl.BlockSpec((None, None, PAST_LEN, D_KV), lambda b, i: (b, layer, 0, 0)),
                  pl.BlockSpec((None, None, PAST_LEN, D_KV), lambda b, i: (b, layer, 0, 0)),
                  pl.BlockSpec((tq, HEAD_DIM), lambda b, i: (i, 0)),
                  pl.BlockSpec((tq, HEAD_DIM), lambda b, i: (i, 0)),
                  pl.BlockSpec((DEC_SEQ, HEAD_DIM), lambda b, i: (0, 0)),
                  pl.BlockSpec((DEC_SEQ, HEAD_DIM), lambda b, i: (0, 0))] + gains
                 + [pl.BlockSpec(memory_space=pl.ANY)],
        out_specs=pl.BlockSpec((tq, D_ATTN), lambda b, i: ((lat0 + b) * nq + i, 0)),
        out_shape=jax.ShapeDtypeStruct((T_ALL, D_ATTN), BF),
        input_output_aliases={11: 0},
        compiler_params=_params(2),
        name="attn_lat",
    )(z, z, z, cache_k, cache_v, cos_t, sin_t, cos_t, sin_t, qg, kg, attn)
    return attn, k_ctx


def _lru_kernel(xr_ref, xg_ref, h0_ref, cw_ref, cb_ref, gw_ref, gb_ref, lam_ref, *rest, seq, ct):
    o_ref, hf_ref, a_s, b_s, hfw_s, hbw_s = rest[-6:]
    n_blocks = ct // RNN_BLOCK
    n_tiles = seq // SUBLANES
    row = lax.broadcasted_iota(jnp.int32, (seq, ct), 0)
    xr = xr_ref[...]

    def shifted(x, k):
        if k > 0:
            return jnp.where(row >= k, pltpu.roll(x, k, 0), 0.0)
        return jnp.where(row < seq + k, pltpu.roll(x, seq + k, 0), 0.0)

    cw = cw_ref[...]
    xc = (cb_ref[...] + cw[0:1] * shifted(xr, 2) + cw[1:2] * shifted(xr, 1) + cw[2:3] * xr
          + cw[3:4] * shifted(xr, -1))

    for kb in range(n_blocks):
        cs = slice(kb * RNN_BLOCK, (kb + 1) * RNN_BLOCK)
        xcb = xc[:, cs]
        gates = jnp.dot(xcb.astype(BF), gw_ref[kb], preferred_element_type=F32) + gb_ref[kb]
        for d in range(2):
            base = d * 2 * RNN_BLOCK
            r = jax.nn.sigmoid(gates[:, base:base + RNN_BLOCK])
            ig = jax.nn.sigmoid(gates[:, base + RNN_BLOCK:base + 2 * RNN_BLOCK])
            lam = lam_ref[kb, d:d + 1, :]
            a = jnp.exp(r * (-LRU_C * jnp.log1p(jnp.exp(-lam))))
            a_s[d, :, cs] = a
            om = 1.0 - a * a
            root = jnp.where(om > 0.0, om * lax.rsqrt(om), 0.0)
            b_s[d, :, cs] = root * (ig * xcb)

    sub = lax.broadcasted_iota(jnp.int32, (SUBLANES, ct), 0)

    def tile_scan(a, b, forward):
        for k in (1, 2, 4):
            if forward:
                keep = sub >= k
                shift = k
            else:
                keep = sub < SUBLANES - k
                shift = SUBLANES - k
            a_n = jnp.where(keep, pltpu.roll(a, shift, 0), 1.0)
            b_n = jnp.where(keep, pltpu.roll(b, shift, 0), 0.0)
            b = a * b_n + b
            a = a * a_n
        return a, b

    def body(j, carry):
        hf, hb = carry
        r0 = pl.multiple_of(j * SUBLANES, SUBLANES)
        a, b = tile_scan(a_s[0, pl.ds(r0, SUBLANES), :], b_s[0, pl.ds(r0, SUBLANES), :], True)
        h = a * hf + b
        hfw_s[pl.ds(r0, SUBLANES), :] = h
        hf = h[SUBLANES - 1:SUBLANES, :]
        r1 = pl.multiple_of((n_tiles - 1 - j) * SUBLANES, SUBLANES)
        a, b = tile_scan(a_s[1, pl.ds(r1, SUBLANES), :], b_s[1, pl.ds(r1, SUBLANES), :], False)
        h = a * hb + b
        hbw_s[pl.ds(r1, SUBLANES), :] = h
        hb = h[0:1, :]
        return hf, hb

    hf, hb = lax.fori_loop(0, n_tiles, body, (h0_ref[0:1, :], h0_ref[1:2, :]))
    hf_ref[0:1, :] = hf
    hf_ref[1:2, :] = hb
    o_ref[...] = (_gelu(xg_ref[...]) * (hfw_s[...] + hbw_s[...])).astype(BF)


def _lru_call(z, h0, cw, cb, gw, gb, lam, layer, *, n_seq, seq, tok0, ct, filled=None):
    nc = D_RNN // ct
    s0 = tok0 // seq
    blk = ct // RNN_BLOCK
    alias = {} if filled is None else dict(input_output_aliases={8: 0})
    alias_spec = [] if filled is None else [pl.BlockSpec(memory_space=pl.ANY)]
    alias_arg = [] if filled is None else [filled]
    return pl.pallas_call(
        functools.partial(_lru_kernel, seq=seq, ct=ct),
        grid=(n_seq, nc),
        in_specs=[pl.BlockSpec((seq, ct), lambda s, c: (s0 + s, OFF_XR // ct + c)),
                  pl.BlockSpec((seq, ct), lambda s, c: (s0 + s, OFF_XG // ct + c)),
                  pl.BlockSpec((None, 2, ct), lambda s, c: (s, 0, c)),
                  pl.BlockSpec((None, RNN_CONV_W, ct), lambda s, c: (layer, 0, c)),
                  pl.BlockSpec((None, 1, ct), lambda s, c: (layer, 0, c)),
                  pl.BlockSpec((None, blk, RNN_BLOCK, 4 * RNN_BLOCK), lambda s, c: (layer, c, 0, 0)),
                  pl.BlockSpec((None, blk, 1, 4 * RNN_BLOCK), lambda s, c: (layer, c, 0, 0)),
                  pl.BlockSpec((None, blk, 2, RNN_BLOCK), lambda s, c: (layer, c, 0, 0))] + alias_spec,
        out_specs=[pl.BlockSpec((seq, ct), lambda s, c: (s0 + s, c)),
                   pl.BlockSpec((None, 2, ct), lambda s, c: (s, 0, c))],
        out_shape=[jax.ShapeDtypeStruct((T_ALL, D_RNN), BF),
                   jax.ShapeDtypeStruct((n_seq, 2, D_RNN), F32)],
        scratch_shapes=[pltpu.VMEM((2, seq, ct), F32), pltpu.VMEM((2, seq, ct), F32),
                        pltpu.VMEM((seq, ct), F32), pltpu.VMEM((seq, ct), F32)],
        compiler_params=_params(2),
        name="rg_lru",
        **alias,
    )(z, z, h0, cw, cb, gw, gb, lam, *alias_arg)


def _s5_disc_kernel(are_ref, aim_ref, ldt_ref, bre_ref, bim_ref, abr_ref, abi_ref, bbr_ref, bbi_ref):
    lam_r = jnp.minimum(are_ref[...], -1e-4)
    lam_i = aim_ref[...]
    dt = jnp.exp(ldt_ref[...])
    mag = jnp.exp(lam_r * dt)
    ab_r = mag * jnp.cos(lam_i * dt)
    ab_i = mag * jnp.sin(lam_i * dt)
    den = lam_r * lam_r + lam_i * lam_i
    q_r = ((ab_r - 1.0) * lam_r + ab_i * lam_i) / den
    q_i = (ab_i * lam_r - (ab_r - 1.0) * lam_i) / den
    abr_ref[...] = ab_r
    abi_ref[...] = ab_i
    br = bre_ref[...]
    bi = bim_ref[...]
    bbr_ref[...] = q_r[:, None, :] * br - q_i[:, None, :] * bi
    bbi_ref[...] = q_r[:, None, :] * bi + q_i[:, None, :] * br


def _s5_discretize(a_re, a_im, log_dt, b_re, b_im):
    n = DEPTH * 2 * N_SSM_GROUPS
    bt_re = jnp.swapaxes(b_re, -1, -2).reshape(n, SSM_GROUP, SSM_STATE)
    bt_im = jnp.swapaxes(b_im, -1, -2).reshape(n, SSM_GROUP, SSM_STATE)
    vec = jax.ShapeDtypeStruct((n, SSM_STATE), F32)
    mat = jax.ShapeDtypeStruct((n, SSM_GROUP, SSM_STATE), F32)
    return pl.pallas_call(
        _s5_disc_kernel,
        out_shape=[vec, vec, mat, mat],
        compiler_params=pltpu.CompilerParams(vmem_limit_bytes=VMEM_LIMIT),
        name="s5_discretize",
    )(a_re.reshape(n, SSM_STATE), a_im.reshape(n, SSM_STATE), log_dt.reshape(n, 1), bt_re, bt_im)


def _s5_kernel(u_ref, h0r_ref, h0i_ref, ar_ref, ai_ref, wb_ref, wcr_ref, wci_ref, d_ref, *rest, seq):
    y_ref, fr_ref, fi_ref, lhs_a, lhs_b, bu_s, h_s, acc_s = rest[-8:]
    rows = S5_CHUNK * SUBLANES
    n_chunks = seq // S5_CHUNK

    @pl.when((pl.program_id(0) == 0) & (pl.program_id(1) == 0))
    def _():
        lhs_a[...] = jnp.zeros_like(lhs_a)
        lhs_b[...] = jnp.zeros_like(lhs_b)

    for b in range(S5_SEQS):
        lhs_a[pl.ds(b, seq, stride=SUBLANES), :] = u_ref[b, :, 0:LANES]
        lhs_b[pl.ds(S5_SEQS + b, seq, stride=SUBLANES), :] = u_ref[b, :, LANES:2 * LANES]

    rid = lax.broadcasted_iota(jnp.int32, (rows, LANES), 0)
    first_tile = (rid & (SUBLANES - 1)) < S5_SEQS

    for d in range(2):
        ar = ar_ref[d]
        ai = ai_ref[d]

        def chunk(c, carry, d=d, ar=ar, ai=ai):
            cc = c if d == 0 else n_chunks - 1 - c
            r0 = pl.multiple_of(cc * rows, rows)
            lhs = jnp.concatenate([lhs_a[pl.ds(r0, rows), :], lhs_b[pl.ds(r0, rows), :]], axis=1).astype(BF)
            bu_s[...] = jnp.dot(lhs, wb_ref[d], preferred_element_type=F32)

            def step(s, hc):
                hr, hi = hc
                ss = s if d == 0 else S5_CHUNK - 1 - s
                q0 = pl.multiple_of(ss * SUBLANES, SUBLANES)
                br = bu_s[pl.ds(q0, SUBLANES), 0:S5_STATES]
                bi = bu_s[pl.ds(q0, SUBLANES), S5_STATES:2 * S5_STATES]
                nr = ar * hr - ai * hi + br
                ni = ar * hi + ai * hr + bi
                h_s[pl.ds(q0, SUBLANES), 0:S5_STATES] = nr
                h_s[pl.ds(q0, SUBLANES), S5_STATES:2 * S5_STATES] = ni
                return nr, ni

            carry = lax.fori_loop(0, S5_CHUNK, step, carry, unroll=8)
            y = (jnp.dot(h_s[:, 0:S5_STATES].astype(BF), wcr_ref[d], preferred_element_type=F32)
                 - jnp.dot(h_s[:, S5_STATES:2 * S5_STATES].astype(BF), wci_ref[d],
                           preferred_element_type=F32))
            y = jnp.where(first_tile, y[:, 0:LANES], y[:, LANES:2 * LANES])
            if d == 0:
                acc_s[pl.ds(r0, rows), :] = y
            else:
                acc_s[pl.ds(r0, rows), :] += y
            return carry

        hr, hi = lax.fori_loop(0, n_chunks, chunk, (h0r_ref[d], h0i_ref[d]))
        fr_ref[d] = hr
        fi_ref[d] = hi

    for b in range(S5_SEQS):
        y = jnp.concatenate([acc_s[pl.ds(b, seq, stride=SUBLANES), :],
                             acc_s[pl.ds(S5_SEQS + b, seq, stride=SUBLANES), :]], axis=1)
        y_ref[b] = _gelu(y + d_ref[...] * u_ref[b])


def _s5_call(us, h0r, h0i, ab_r, ab_i, wb, wcr, wci, d_skip, layer, *, n_blk, seq, row0, filled=None):
    blk0 = row0 // seq
    st = (None, None, 2, SUBLANES, S5_STATES)
    tok_spec = pl.BlockSpec((S5_SEQS, seq, 2 * LANES), lambda b, p: (0, blk0 + b, p))
    h0_spec = pl.BlockSpec(st, lambda b, p: (0, p, 0, 0, 0))
    par_spec = pl.BlockSpec(st, lambda b, p: (layer, p, 0, 0, 0))
    fin_spec = pl.BlockSpec(st, lambda b, p: (b, p, 0, 0, 0))
    fin_shape = jax.ShapeDtypeStruct((n_blk, S5_PAIRS, 2, SUBLANES, S5_STATES), F32)
    alias = {} if filled is None else dict(input_output_aliases={9: 0})
    alias_spec = [] if filled is None else [pl.BlockSpec(memory_space=pl.ANY)]
    alias_arg = [] if filled is None else [filled]
    return pl.pallas_call(
        functools.partial(_s5_kernel, seq=seq),
        grid=(n_blk, S5_PAIRS),
        in_specs=[tok_spec, h0_spec, h0_spec, par_spec, par_spec,
                  pl.BlockSpec((None, None, 2, 2 * LANES, 2 * S5_STATES), lambda b, p: (layer, p, 0, 0, 0)),
                  pl.BlockSpec((None, None, 2, S5_STATES, 2 * LANES), lambda b, p: (layer, p, 0, 0, 0)),
                  pl.BlockSpec((None, None, 2, S5_STATES, 2 * LANES), lambda b, p: (layer, p, 0, 0, 0)),
                  pl.BlockSpec((None, 1, 2 * LANES), lambda b, p: (layer, 0, p))] + alias_spec,
        out_specs=[tok_spec, fin_spec, fin_spec],
        out_shape=[jax.ShapeDtypeStruct((S5_SEQS, S5_ROWS, D_SSM), F32), fin_shape, fin_shape],
        scratch_shapes=[pltpu.VMEM((seq * SUBLANES, LANES), F32), pltpu.VMEM((seq * SUBLANES, LANES), F32),
                        pltpu.VMEM((S5_CHUNK * SUBLANES, 2 * S5_STATES), F32),
                        pltpu.VMEM((S5_CHUNK * SUBLANES, 2 * S5_STATES), F32),
                        pltpu.VMEM((seq * SUBLANES, LANES), F32)],
        compiler_params=_params(2),
        name="s5_scan",
        **alias,
    )(us, h0r, h0i, ab_r, ab_i, wb, wcr, wci, d_skip, *alias_arg)


def _s5_pair_rows(st):
    x = st.reshape(S5_SEQS, 2, S5_PAIRS, 2, S5_STATES)
    return x.transpose(2, 1, 3, 0, 4).reshape(1, S5_PAIRS, 2, SUBLANES, S5_STATES)


def _s5_pair_unrows(st, n_seq):
    x = st.reshape(n_seq // S5_SEQS, S5_PAIRS, 2, 2, S5_SEQS, S5_STATES)
    return x.transpose(0, 4, 2, 1, 3, 5).reshape(n_seq, 2, N_SSM_GROUPS, SSM_STATE)


def _glu_kernel(s_ref, w_ref, b_ref, o_ref, wbf):
    @pl.when(pl.program_id(0) == 0)
    def _():
        wbf[...] = w_ref[...].astype(BF)

    ssm = s_ref[...]
    gate = jnp.dot(ssm.astype(BF), wbf[...], preferred_element_type=F32) + b_ref[...]
    o_ref[...] = (ssm * jax.nn.sigmoid(gate)).astype(BF)


def _s5_glu(ssm, glu_w, glu_b, layer):
    def seg_index(i):
        slot, rblk = _segment_block(i)
        return slot, rblk, 0

    vec = pl.BlockSpec((None, 1, D_SSM), lambda i: (layer, 0, 0))
    return pl.pallas_call(
        _glu_kernel,
        grid=(T_ALL // SEQ,),
        in_specs=[pl.BlockSpec((None, SEQ, D_SSM), seg_index),
                  pl.BlockSpec((None, D_SSM, D_SSM), lambda i: (layer, 0, 0)), vec],
        out_specs=pl.BlockSpec((SEQ, D_SSM), lambda i: (i, 0)),
        out_shape=jax.ShapeDtypeStruct((T_ALL, D_SSM), BF),
        scratch_shapes=[pltpu.VMEM((D_SSM, D_SSM), BF)],
        compiler_params=_params(1),
        name="s5_glu",
    )(ssm, glu_w, glu_b.reshape(DEPTH, 1, D_SSM))


def _rope_tables():
    rows = DEC_SEQ // GRID_W
    t_row = jnp.repeat(jnp.arange(rows, dtype=F32), GRID_W)
    t_col = (jnp.arange(DEC_SEQ) % GRID_W).astype(F32)
    inv = 1.0 / (ROPE_THETA ** (jnp.arange(0, AXIS_DIM, 2, dtype=F32) / AXIS_DIM))
    ang_r = t_row[:, None] * inv
    ang_c = t_col[:, None] * inv
    cos = jnp.concatenate([jnp.cos(ang_r), jnp.cos(ang_r), jnp.cos(ang_c), jnp.cos(ang_c)], axis=-1)
    sin = jnp.concatenate([-jnp.sin(ang_r), jnp.sin(ang_r), -jnp.sin(ang_c), jnp.sin(ang_c)], axis=-1)
    return cos, sin


def _block_diag(blocks):
    n, r, c = blocks.shape[-3:]
    eye = jnp.eye(n, dtype=blocks.dtype)
    out = blocks[..., :, :, None, :] * eye[:, None, :, None]
    return out.reshape(blocks.shape[:-3] + (n * r, n * c))


def kernel(x_prompt, x_sample, c, cache_k, cache_v, state_lru, state_s5_re, state_s5_im, c_ctx, w_mod, b_mod, g_mix, w_in, q_gain, k_gain, rnn_conv_w, rnn_conv_b, lru_gate_w, lru_gate_b, lru_lambda, s5_a_re, s5_a_im, s5_log_dt, s5_b_re, s5_b_im, s5_c_re, s5_c_im, s5_d, glu_w, glu_b, w_pa, w_pr, w_ps, w_o, g_ffn, w_up, ffn_conv_w, ffn_conv_b, w_down):
    x = jnp.concatenate([x_prompt.reshape(T_CTX, D_MODEL), x_sample.reshape(T_LAT, D_MODEL)], axis=0)

    cvec = jnp.zeros((MOD_ROWS, D_MODEL), F32).at[0].set(c_ctx).at[1:1 + DEC_BATCH].set(c)
    mod = _modulation(cvec, w_mod, b_mod)
    cos_t, sin_t = _rope_tables()

    gw = lru_gate_w.transpose(0, 3, 4, 1, 2, 5).reshape(DEPTH, N_RNN_BLOCKS, RNN_BLOCK, 4 * RNN_BLOCK).astype(BF)
    gb = (lru_gate_b.reshape(DEPTH, 2, 2, N_RNN_BLOCKS, RNN_BLOCK).transpose(0, 3, 1, 2, 4)
          .reshape(DEPTH, N_RNN_BLOCKS, 1, 4 * RNN_BLOCK))
    lam = lru_lambda.reshape(DEPTH, 2, N_RNN_BLOCKS, RNN_BLOCK).transpose(0, 2, 1, 3)
    cb = rnn_conv_b.reshape(DEPTH, 1, D_RNN)

    ab_r, ab_i, bbt_r, bbt_i = _s5_discretize(s5_a_re, s5_a_im, s5_log_dt, s5_b_re, s5_b_im)
    def pair_rows(ab):
        x = ab.reshape(DEPTH, 2, S5_PAIRS, 2, 1, S5_STATES)
        x = jnp.broadcast_to(x, (DEPTH, 2, S5_PAIRS, 2, S5_SEQS, S5_STATES))
        return x.transpose(0, 2, 1, 3, 4, 5).reshape(DEPTH, S5_PAIRS, 2, SUBLANES, S5_STATES)

    ab_r = pair_rows(ab_r)
    ab_i = pair_rows(ab_i)
    gshape = (DEPTH, 2, S5_TILES, S5_GROUPS, SSM_GROUP, SSM_STATE)
    wb = jnp.concatenate([_block_diag(bbt_r.reshape(gshape)), _block_diag(bbt_i.reshape(gshape))], axis=-1)
    wb = wb.reshape(DEPTH, 2, S5_PAIRS, 2 * LANES, 2 * S5_STATES)
    wb = wb.transpose(0, 2, 1, 3, 4).astype(BF)

    def pair_cols(c):
        x = _block_diag(jnp.swapaxes(c.reshape(gshape), -1, -2))
        x = x.reshape(DEPTH, 2, S5_PAIRS, 2, S5_STATES, LANES)
        return x.transpose(0, 2, 1, 4, 3, 5).reshape(DEPTH, S5_PAIRS, 2, S5_STATES, 2 * LANES).astype(BF)

    wcr = pair_cols(s5_c_re)
    wci = pair_cols(s5_c_im)
    d_skip = s5_d.reshape(DEPTH, 1, D_SSM)

    zeros_lru = jnp.zeros((BATCH, 2, D_RNN), F32)
    zeros_s5 = jnp.zeros((1, S5_PAIRS, 2, SUBLANES, S5_STATES), F32)
    cache_k = cache_k.reshape(DEC_BATCH, DEPTH, PAST_LEN, D_KV)
    cache_v = cache_v.reshape(DEC_BATCH, DEPTH, PAST_LEN, D_KV)

    new_k, new_v, new_lru, new_sr, new_si = [], [], [], [], []
    for l in range(DEPTH):
        h = _norm_mod(x, g_mix, mod, l, 0, 1)
        z, z_gate, us = _in_proj(h, w_in, l)

        attn, k_ctx = _attention(z, q_gain, k_gain, cache_k, cache_v, cos_t, sin_t, l)
        new_k.append(k_ctx.reshape(BATCH, SEQ, N_KV_HEADS, HEAD_DIM))
        new_v.append(z[:T_CTX, OFF_V:OFF_V + D_KV].reshape(BATCH, SEQ, N_KV_HEADS, HEAD_DIM))

        rec, lru_f = _lru_call(z, zeros_lru, rnn_conv_w, cb, gw, gb, lam, l,
                               n_seq=BATCH, seq=SEQ, tok0=0, ct=512)
        rec, _ = _lru_call(z, state_lru[:, l], rnn_conv_w, cb, gw, gb, lam, l,
                           n_seq=DEC_BATCH, seq=DEC_SEQ, tok0=T_CTX, ct=256, filled=rec)
        new_lru.append(lru_f)

        s5, f_r, f_i = _s5_call(us, zeros_s5, zeros_s5, ab_r, ab_i, wb, wcr, wci, d_skip, l,
                                n_blk=BATCH // S5_SEQS, seq=SEQ, row0=0)
        s5, _, _ = _s5_call(us, _s5_pair_rows(state_s5_re[:, l]), _s5_pair_rows(state_s5_im[:, l]),
                            ab_r, ab_i, wb, wcr, wci, d_skip, l,
                            n_blk=DEC_BATCH // S5_SEQS, seq=DEC_SEQ, row0=S5_LAT_ROW0, filled=s5)
        ssm = _s5_glu(s5, glu_w, glu_b, l)
        new_sr.append(_s5_pair_unrows(f_r, BATCH))
        new_si.append(_s5_pair_unrows(f_i, BATCH))

        merged = _merge(attn, rec, ssm, z_gate, w_pa, w_pr, w_ps, l)
        x = _residual_proj(merged, w_o, x, mod, l, 2, "out_proj")

        h2 = _norm_mod(x, g_ffn, mod, l, 3, 4)
        hmid = _ffn_up(h2, w_up, ffn_conv_w, ffn_conv_b, l)
        x = _residual_proj(hmid, w_down, x, mod, l, 5, "ffn_down")

    return (x[:T_CTX].reshape(BATCH, SEQ, D_MODEL), x[T_CTX:].reshape(DEC_BATCH, DEC_SEQ, D_MODEL),
            jnp.stack(new_k, axis=1), jnp.stack(new_v, axis=1), jnp.stack(new_lru, axis=1),
            jnp.stack(new_sr, axis=1), jnp.stack(new_si, axis=1))
```

```python
import functools
import math

import jax
import jax.numpy as jnp
from jax import lax
from jax.experimental import pallas as pl
from jax.experimental.pallas import tpu as pltpu

F32 = jnp.float32
BF = jnp.bfloat16

D_MODEL = 2048
BATCH = 32
SEQ = 256
DEPTH = 4
DEC_BATCH = 4
DEC_SEQ = 1024
PAST_LEN = 256
GRID_W = 64
N_HEADS = 8
N_KV_HEADS = 2
HEAD_DIM = 128
AXIS_DIM = HEAD_DIM // 2
ROPE_THETA = 10000.0
D_ATTN = N_HEADS * HEAD_DIM
D_KV = N_KV_HEADS * HEAD_DIM
D_RNN = D_MODEL // 2
RNN_BLOCK = 128
N_RNN_BLOCKS = D_RNN // RNN_BLOCK
RNN_CONV_W = 4
LRU_C = 8.0
D_SSM = D_MODEL // 2
SSM_GROUP = 16
N_SSM_GROUPS = D_SSM // SSM_GROUP
SSM_STATE = 64
D_FF = 2 * D_MODEL
N_MOD = 6
EPS = 1e-6
D_IN = D_ATTN + 2 * D_KV + 2 * D_RNN + D_SSM + 3 * D_MODEL

OFF_Q = 0
OFF_K = D_ATTN
OFF_V = OFF_K + D_KV
OFF_XR = OFF_V + D_KV
OFF_XG = OFF_XR + D_RNN
OFF_US = OFF_XG + D_RNN
OFF_GL = OFF_US + D_SSM

T_CTX = BATCH * SEQ
T_LAT = DEC_BATCH * DEC_SEQ
T_ALL = T_CTX + T_LAT
MOD_ROWS = 8

SUBLANES = 8
LANES = 128
VMEM_LIMIT = 56 * 1024 * 1024

S5_SEQS = SUBLANES // 2
S5_GROUPS = LANES // SSM_GROUP
S5_TILES = N_SSM_GROUPS // S5_GROUPS
S5_PAIRS = S5_TILES // 2
S5_STATES = S5_GROUPS * SSM_STATE
S5_CHUNK = 128
S5_GROUP = 32
S5_LAT_ROW0 = (BATCH // S5_SEQS) * SEQ
S5_ROWS = S5_LAT_ROW0 + DEC_SEQ


def _params(n_axes):
    return pltpu.CompilerParams(dimension_semantics=("arbitrary",) * n_axes,
                                vmem_limit_bytes=VMEM_LIMIT)


def _mod_row(first_token):
    return jnp.where(first_token < T_CTX, 0, 1 + jnp.maximum(first_token - T_CTX, 0) // DEC_SEQ)


def _gelu(x):
    return jax.nn.gelu(x)


def _mod_kernel(c_ref, w_ref, b_ref, o_ref):
    c = c_ref[...]
    s = (c * jax.nn.sigmoid(c)).astype(BF)
    o_ref[...] = jnp.dot(s, w_ref[...].astype(BF), preferred_element_type=F32) + b_ref[...]


def _modulation(cvec, w_mod, b_mod):
    tn = 1536
    n_out = N_MOD * D_MODEL
    return pl.pallas_call(
        _mod_kernel,
        grid=(DEPTH, n_out // tn),
        in_specs=[pl.BlockSpec((MOD_ROWS, D_MODEL), lambda l, j: (0, 0)),
                  pl.BlockSpec((None, D_MODEL, tn), lambda l, j: (l, 0, j)),
                  pl.BlockSpec((None, 1, tn), lambda l, j: (l, 0, j))],
        out_specs=pl.BlockSpec((None, MOD_ROWS, tn), lambda l, j: (l, 0, j)),
        out_shape=jax.ShapeDtypeStruct((DEPTH, MOD_ROWS, n_out), F32),
        compiler_params=_params(2),
        name="modulation",
    )(cvec, w_mod, b_mod.reshape(DEPTH, 1, n_out))


def _norm_kernel(x_ref, g_ref, sc_ref, sh_ref, o_ref, *, tm):
    r = _mod_row(pl.program_id(0) * tm)
    x = x_ref[...]
    y = x * lax.rsqrt(jnp.mean(x * x, axis=-1, keepdims=True) + EPS)
    y = y * g_ref[...]
    sc = sc_ref[pl.ds(r, 1), :]
    sh = sh_ref[pl.ds(r, 1), :]
    o_ref[...] = (y * (1.0 + sc) + sh).astype(BF)


def _norm_mod(x, gain, mod, layer, shift_idx, scale_idx):
    tm = 512
    return pl.pallas_call(
        functools.partial(_norm_kernel, tm=tm),
        grid=(T_ALL // tm,),
        in_specs=[pl.BlockSpec((tm, D_MODEL), lambda i: (i, 0)),
                  pl.BlockSpec((None, 1, D_MODEL), lambda i: (layer, 0, 0)),
                  pl.BlockSpec((None, MOD_ROWS, D_MODEL), lambda i: (layer, 0, scale_idx)),
                  pl.BlockSpec((None, MOD_ROWS, D_MODEL), lambda i: (layer, 0, shift_idx))],
        out_specs=pl.BlockSpec((tm, D_MODEL), lambda i: (i, 0)),
        out_shape=jax.ShapeDtypeStruct((T_ALL, D_MODEL), BF),
        compiler_params=_params(1),
        name="norm_mod",
    )(x, gain.reshape(DEPTH, 1, D_MODEL), mod, mod)


def _ws_kernel(*refs, n_a, n_w, n_extra, n_out, lhs_of, epilogue, tm):
    a_refs = refs[:n_a]
    w_refs = refs[n_a:n_a + n_w]
    extra = refs[n_a + n_w:n_a + n_w + n_extra]
    outs = refs[n_a + n_w + n_extra:n_a + n_w + n_extra + n_out]
    wbf = refs[n_a + n_w + n_extra + n_out:]
    j = pl.program_id(0)
    i = pl.program_id(1)

    @pl.when(i == 0)
    def _():
        for k in range(n_w):
            wbf[k][...] = w_refs[k][...].astype(BF)

    accs = [jnp.dot(a_refs[lhs_of[k]][...], wbf[k][...], preferred_element_type=F32)
            for k in range(n_w)]
    epilogue(accs, extra, outs, j, i * tm)


def _ws_matmul(a_list, w_list, extra, out_shapes, out_specs, epilogue, *, lhs_of, tm, tn, n_tiles, name):
    in_specs, args, scratch = [], [], []
    for a in a_list:
        in_specs.append(pl.BlockSpec((tm, a.shape[1]), lambda j, i: (i, 0)))
        args.append(a)
    for w, layer, off in w_list:
        kdim = w.shape[1]
        in_specs.append(pl.BlockSpec((None, kdim, tn), lambda j, i, layer=layer, off=off: (layer, 0, off + j)))
        args.append(w)
        scratch.append(pltpu.VMEM((kdim, tn), BF))
    for arr, spec in extra:
        in_specs.append(spec)
        args.append(arr)
    kern = functools.partial(_ws_kernel, n_a=len(a_list), n_w=len(w_list), n_extra=len(extra),
                             n_out=len(out_shapes), lhs_of=lhs_of, epilogue=epilogue, tm=tm)
    return pl.pallas_call(
        kern,
        grid=(n_tiles, T_ALL // tm),
        in_specs=in_specs,
        out_specs=out_specs,
        out_shape=out_shapes,
        scratch_shapes=scratch,
        compiler_params=_params(2),
        name=name,
    )(*args)


def _segment_block(i):
    il = i - BATCH
    per_lat = DEC_SEQ // SEQ
    slot = jnp.where(i < BATCH, i % S5_SEQS, il // per_lat)
    rblk = jnp.where(i < BATCH, i // S5_SEQS, S5_LAT_ROW0 // SEQ + il % per_lat)
    return slot, rblk


def _store_acc(accs, extra, outs, j, row0):
    outs[0][...] = accs[0]


def _in_proj(h, w_in, layer):
    tm = 512
    tn_a = 896
    z_a = _ws_matmul([h], [(w_in, layer, 0)], [],
                     [jax.ShapeDtypeStruct((T_ALL, OFF_US), F32)],
                     [pl.BlockSpec((tm, tn_a), lambda j, i: (i, j))],
                     _store_acc, lhs_of=(0,), tm=tm, tn=tn_a, n_tiles=OFF_US // tn_a, name="in_proj_a")[0]
    tn_g = 1536
    z_g = _ws_matmul([h], [(w_in, layer, OFF_GL // tn_g)], [],
                     [jax.ShapeDtypeStruct((T_ALL, 3 * D_MODEL), F32)],
                     [pl.BlockSpec((tm, tn_g), lambda j, i: (i, j))],
                     _store_acc, lhs_of=(0,), tm=tm, tn=tn_g, n_tiles=3 * D_MODEL // tn_g, name="in_proj_g")[0]

    tn_u = 512

    def us_index(j, i):
        slot, rblk = _segment_block(i)
        return slot, rblk, 0

    def us_epilogue(accs, extra, outs, j, row0):
        outs[0][...] = jnp.concatenate(accs, axis=1)

    us = _ws_matmul([h], [(w_in, layer, OFF_US // tn_u), (w_in, layer, OFF_US // tn_u + 1)], [],
                    [jax.ShapeDtypeStruct((S5_SEQS, S5_ROWS, D_SSM), F32)],
                    [pl.BlockSpec((None, SEQ, D_SSM), us_index)],
                    us_epilogue, lhs_of=(0, 0), tm=SEQ, tn=tn_u, n_tiles=1, name="in_proj_u")[0]
    return z_a, z_g, us


def _merge(attn, rec, ssm, z, w_pa, w_pr, w_ps, layer):
    tm, tn = 512, 512
    gl0 = 0

    def epilogue(accs, extra, outs, j, row0):
        m = jax.nn.sigmoid(extra[0][...]) * accs[0]
        m = m + jax.nn.sigmoid(extra[1][...]) * accs[1]
        m = m + jax.nn.sigmoid(extra[2][...]) * accs[2]
        outs[0][...] = m.astype(BF)

    gates = [(z, pl.BlockSpec((tm, tn), lambda j, i, b=b: (i, gl0 + b * (D_MODEL // tn) + j)))
             for b in range(3)]
    return _ws_matmul([attn, rec, ssm], [(w_pa, layer, 0), (w_pr, layer, 0), (w_ps, layer, 0)], gates,
                      [jax.ShapeDtypeStruct((T_ALL, D_MODEL), BF)],
                      [pl.BlockSpec((tm, tn), lambda j, i: (i, j))],
                      epilogue, lhs_of=(0, 1, 2), tm=tm, tn=tn, n_tiles=D_MODEL // tn, name="merge")[0]


def _residual_kernel(a_ref, w_ref, x_ref, gate_ref, *rest, tm, with_norm):
    r = _mod_row(pl.program_id(0) * tm)
    acc = jnp.dot(a_ref[...], w_ref[...], preferred_element_type=F32)
    x = x_ref[...] + gate_ref[pl.ds(r, 1), :] * acc
    if not with_norm:
        rest[0][...] = x
        return
    g_ref, sc_ref, sh_ref, x_out, h_out = rest
    x_out[...] = x
    y = x * lax.rsqrt(jnp.mean(x * x, axis=-1, keepdims=True) + EPS)
    y = y * g_ref[...]
    h_out[...] = (y * (1.0 + sc_ref[pl.ds(r, 1), :]) + sh_ref[pl.ds(r, 1), :]).astype(BF)


def _residual_proj(a, w_bf, x, mod, layer, gate_idx, name, *, tm, norm=None):
    kdim = a.shape[1]
    row = pl.BlockSpec((tm, D_MODEL), lambda i: (i, 0))
    in_specs = [pl.BlockSpec((tm, kdim), lambda i: (i, 0)),
                pl.BlockSpec((None, kdim, D_MODEL), lambda i: (layer, 0, 0), pipeline_mode=pl.Buffered(1)),
                row,
                pl.BlockSpec((None, MOD_ROWS, D_MODEL), lambda i: (layer, 0, gate_idx))]
    args = [a, w_bf, x, mod]
    out_shape = [jax.ShapeDtypeStruct((T_ALL, D_MODEL), F32)]
    out_specs = [row]
    if norm is not None:
        gain, nl, shift_idx, scale_idx = norm
        in_specs += [pl.BlockSpec((None, 1, D_MODEL), lambda i: (nl, 0, 0)),
                     pl.BlockSpec((None, MOD_ROWS, D_MODEL), lambda i: (nl, 0, scale_idx)),
                     pl.BlockSpec((None, MOD_ROWS, D_MODEL), lambda i: (nl, 0, shift_idx))]
        args += [gain.reshape(DEPTH, 1, D_MODEL), mod, mod]
        out_shape.append(jax.ShapeDtypeStruct((T_ALL, D_MODEL), BF))
        out_specs.append(row)
    return pl.pallas_call(
        functools.partial(_residual_kernel, tm=tm, with_norm=norm is not None),
        grid=(T_ALL // tm,),
        in_specs=in_specs,
        out_specs=out_specs,
        out_shape=out_shape,
        compiler_params=_params(1),
        name=name,
    )(*args)


def _ffn_up(h2, w_up, conv_w, conv_b, layer):
    tm, tn = DEC_SEQ, 512

    def epilogue(accs, extra, outs, j, row0):
        a, b = accs
        seq_len = jnp.where(row0 < T_CTX, SEQ, DEC_SEQ)
        pos = lax.broadcasted_iota(jnp.int32, a.shape, 0) & (seq_len - 1)
        prev = jnp.where(pos == 0, 0.0, pltpu.roll(a, 1, 0))
        nxt = jnp.where(pos == seq_len - 1, 0.0, pltpu.roll(a, tm - 1, 0))
        cw = extra[0][...]
        a = cw[0:1] * prev + cw[1:2] * a + cw[2:3] * nxt + extra[1][...]
        outs[0][...] = (_gelu(a) * b).astype(BF)

    extra = [(conv_w, pl.BlockSpec((None, 3, tn), lambda j, i: (layer, 0, j))),
             (conv_b.reshape(DEPTH, 1, D_FF), pl.BlockSpec((None, 1, tn), lambda j, i: (layer, 0, j)))]
    return _ws_matmul([h2], [(w_up, layer, 0), (w_up, layer, D_FF // tn)], extra,
                      [jax.ShapeDtypeStruct((T_ALL, D_FF), BF)],
                      [pl.BlockSpec((tm, tn), lambda j, i: (i, j))],
                      epilogue, lhs_of=(0, 0), tm=tm, tn=tn, n_tiles=D_FF // tn, name="ffn_up")[0]


def _head_norm(x, gain):
    return x * lax.rsqrt(jnp.mean(x * x, axis=-1, keepdims=True) + EPS) * gain


def _rope(x, cos, sin_signed):
    lane = lax.broadcasted_iota(jnp.int32, x.shape, 1)
    first_half = (lane & (AXIS_DIM - 1)) < AXIS_DIM // 2
    partner = jnp.where(first_half, pltpu.roll(x, HEAD_DIM - AXIS_DIM // 2, 1),
                        pltpu.roll(x, AXIS_DIM // 2, 1))
    return x * cos + partner * sin_signed


def _scores(q, k):
    return lax.dot_general(q, k, (((1,), (1,)), ((), ())), preferred_element_type=F32) * (HEAD_DIM ** -0.5)


def _attn_ctx_kernel(q_ref, k_ref, v_ref, qg_ref, kg_ref, o_ref, ko_ref):
    grp = N_HEADS // N_KV_HEADS
    for kv in range(N_KV_HEADS):
        ks = slice(kv * HEAD_DIM, (kv + 1) * HEAD_DIM)
        kn = _head_norm(k_ref[:, ks], kg_ref[...])
        ko_ref[:, ks] = kn
        kn = kn.astype(BF)
        vv = v_ref[:, ks].astype(BF)
        for g in range(grp):
            hs = slice((kv * grp + g) * HEAD_DIM, (kv * grp + g + 1) * HEAD_DIM)
            qn = _head_norm(q_ref[:, hs], qg_ref[...]).astype(BF)
            s = _scores(qn, kn)
            p = jnp.exp(s - jnp.max(s, axis=-1, keepdims=True))
            den = jnp.sum(p, axis=-1, keepdims=True)
            o = jnp.dot(p.astype(BF), vv, preferred_element_type=F32) / den
            o_ref[:, hs] = o.astype(BF)


def _attn_lat_kernel(q_ref, k_ref, v_ref, ck_ref, cv_ref, cosq_ref, sinq_ref, cosk_ref, sink_ref,
                     qg_ref, kg_ref, filled_ref, o_ref):
    del filled_ref
    grp = N_HEADS // N_KV_HEADS
    for kv in range(N_KV_HEADS):
        ks = slice(kv * HEAD_DIM, (kv + 1) * HEAD_DIM)
        kn = _rope(_head_norm(k_ref[:, ks], kg_ref[...]), cosk_ref[...], sink_ref[...]).astype(BF)
        vv = v_ref[:, ks].astype(BF)
        ck = ck_ref[:, ks].astype(BF)
        cv = cv_ref[:, ks].astype(BF)
        for g in range(grp):
            hs = slice((kv * grp + g) * HEAD_DIM, (kv * grp + g + 1) * HEAD_DIM)
            qn = _rope(_head_norm(q_ref[:, hs], qg_ref[...]), cosq_ref[...], sinq_ref[...]).astype(BF)
            s1 = _scores(qn, ck)
            s2 = _scores(qn, kn)
            m = jnp.maximum(jnp.max(s1, axis=-1, keepdims=True), jnp.max(s2, axis=-1, keepdims=True))
            p1 = jnp.exp(s1 - m)
            p2 = jnp.exp(s2 - m)
            den = jnp.sum(p1, axis=-1, keepdims=True) + jnp.sum(p2, axis=-1, keepdims=True)
            o = (jnp.dot(p1.astype(BF), cv, preferred_element_type=F32)
                 + jnp.dot(p2.astype(BF), vv, preferred_element_type=F32)) / den
            o_ref[:, hs] = o.astype(BF)


def _attention(z, q_gain, k_gain, cache_k, cache_v, cos_t, sin_t, layer):
    gains = [pl.BlockSpec((None, 1, HEAD_DIM), lambda *_: (layer, 0, 0))] * 2
    qg = q_gain.reshape(DEPTH, 1, HEAD_DIM)
    kg = k_gain.reshape(DEPTH, 1, HEAD_DIM)
    kcol, vcol = OFF_K // D_KV, OFF_V // D_KV
    attn, k_ctx = pl.pallas_call(
        _attn_ctx_kernel,
        grid=(BATCH,),
        in_specs=[pl.BlockSpec((SEQ, D_ATTN), lambda b: (b, 0)),
                  pl.BlockSpec((SEQ, D_KV), lambda b: (b, kcol)),
                  pl.BlockSpec((SEQ, D_KV), lambda b: (b, vcol))] + gains,
        out_specs=[pl.BlockSpec((SEQ, D_ATTN), lambda b: (b, 0)),
                   pl.BlockSpec((SEQ, D_KV), lambda b: (b, 0))],
        out_shape=[jax.ShapeDtypeStruct((T_ALL, D_ATTN), BF),
                   jax.ShapeDtypeStruct((T_CTX, D_KV), F32)],
        compiler_params=_params(1),
        name="attn_ctx",
    )(z, z, z, qg, kg)

    tq = 512
    nq = DEC_SEQ // tq
    lat0 = T_CTX // DEC_SEQ
    attn = pl.pallas_call(
        _attn_lat_kernel,
        grid=(DEC_BATCH, nq),
        in_specs=[pl.BlockSpec((tq, D_ATTN), lambda b, i: ((lat0 + b) * nq + i, 0)),
                  pl.BlockSpec((DEC_SEQ, D_KV), lambda b, i: (lat0 + b, kcol)),
                  pl.BlockSpec((DEC_SEQ, D_KV), lambda b, i: (lat0 + b, vcol)),
                  pl.BlockSpec((None, None, PAST_LEN, D_KV), lambda b, i: (b, layer, 0, 0)),
                  pl.BlockSpec((None, None, PAST_LEN, D_KV), lambda b, i: (b, layer, 0, 0)),
                  pl.BlockSpec((tq, HEAD_DIM), lambda b, i: (i, 0)),
                  pl.BlockSpec((tq, HEAD_DIM), lambda b, i: (i, 0)),
                  pl.BlockSpec((DEC_SEQ, HEAD_DIM), lambda b, i: (0, 0)),
                  pl.BlockSpec((DEC_SEQ, HEAD_DIM), lambda b, i: (0, 0))] + gains
                 + [pl.BlockSpec(memory_space=pl.ANY)],
        out_specs=pl.BlockSpec((tq, D_ATTN), lambda b, i: ((lat0 + b) * nq + i, 0)),
        out_shape=jax.ShapeDtypeStruct((T_ALL, D_ATTN), BF),
        input_output_aliases={11: 0},
        compiler_params=_params(2),
        name="attn_lat",
    )(z, z, z, cache_k, cache_v, cos_t, sin_t, cos_t, sin_t, qg, kg, attn)
    return attn, k_ctx


def _lru_kernel(xr_ref, xg_ref, h0_ref, cw_ref, cb_ref, gw_ref, gb_ref, lam_ref, *rest, seq, ct):
    o_ref, hf_ref, a_s, b_s, hfw_s, hbw_s = rest[-6:]
    n_blocks = ct // RNN_BLOCK
    n_tiles = seq // SUBLANES
    row = lax.broadcasted_iota(jnp.int32, (seq, ct), 0)
    xr = xr_ref[...]

    def shifted(x, k):
        if k > 0:
            return jnp.where(row >= k, pltpu.roll(x, k, 0), 0.0)
        return jnp.where(row < seq + k, pltpu.roll(x, seq + k, 0), 0.0)

    cw = cw_ref[...]
    xc = (cb_ref[...] + cw[0:1] * shifted(xr, 2) + cw[1:2] * shifted(xr, 1) + cw[2:3] * xr
          + cw[3:4] * shifted(xr, -1))

    for kb in range(n_blocks):
        cs = slice(kb * RNN_BLOCK, (kb + 1) * RNN_BLOCK)
        xcb = xc[:, cs]
        gates = jnp.dot(xcb.astype(BF), gw_ref[kb], preferred_element_type=F32) + gb_ref[kb]
        for d in range(2):
            base = d * 2 * RNN_BLOCK
            r = jax.nn.sigmoid(gates[:, base:base + RNN_BLOCK])
            ig = jax.nn.sigmoid(gates[:, base + RNN_BLOCK:base + 2 * RNN_BLOCK])
            lam = lam_ref[kb, d:d + 1, :]
            a = jnp.exp(r * (-LRU_C * jnp.log1p(jnp.exp(-lam))))
            a_s[d, :, cs] = a
            om = 1.0 - a * a
            root = jnp.where(om > 0.0, om * lax.rsqrt(om), 0.0)
            b_s[d, :, cs] = root * (ig * xcb)

    sub = lax.broadcasted_iota(jnp.int32, (SUBLANES, ct), 0)

    def tile_scan(a, b, forward):
        for k in (1, 2, 4):
            if forward:
                keep = sub >= k
                shift = k
            else:
                keep = sub < SUBLANES - k
                shift = SUBLANES - k
            a_n = jnp.where(keep, pltpu.roll(a, shift, 0), 1.0)
            b_n = jnp.where(keep, pltpu.roll(b, shift, 0), 0.0)
            b = a * b_n + b
            a = a * a_n
        return a, b

    def body(j, carry):
        hf, hb = carry
        r0 = pl.multiple_of(j * SUBLANES, SUBLANES)
        a, b = tile_scan(a_s[0, pl.ds(r0, SUBLANES), :], b_s[0, pl.ds(r0, SUBLANES), :], True)
        h = a * hf + b
        hfw_s[pl.ds(r0, SUBLANES), :] = h
        hf = h[SUBLANES - 1:SUBLANES, :]
        r1 = pl.multiple_of((n_tiles - 1 - j) * SUBLANES, SUBLANES)
        a, b = tile_scan(a_s[1, pl.ds(r1, SUBLANES), :], b_s[1, pl.ds(r1, SUBLANES), :], False)
        h = a * hb + b
        hbw_s[pl.ds(r1, SUBLANES), :] = h
        hb = h[0:1, :]
        return hf, hb

    hf, hb = lax.fori_loop(0, n_tiles, body, (h0_ref[0:1, :], h0_ref[1:2, :]))
    hf_ref[0:1, :] = hf
    hf_ref[1:2, :] = hb
    o_ref[...] = (_gelu(xg_ref[...]) * (hfw_s[...] + hbw_s[...])).astype(BF)


def _lru_call(z, h0, cw, cb, gw, gb, lam, layer, *, n_seq, seq, tok0, ct, filled=None):
    nc = D_RNN // ct
    s0 = tok0 // seq
    blk = ct // RNN_BLOCK
    alias = {} if filled is None else dict(input_output_aliases={8: 0})
    alias_spec = [] if filled is None else [pl.BlockSpec(memory_space=pl.ANY)]
    alias_arg = [] if filled is None else [filled]
    return pl.pallas_call(
        functools.partial(_lru_kernel, seq=seq, ct=ct),
        grid=(n_seq, nc),
        in_specs=[pl.BlockSpec((seq, ct), lambda s, c: (s0 + s, OFF_XR // ct + c)),
                  pl.BlockSpec((seq, ct), lambda s, c: (s0 + s, OFF_XG // ct + c)),
                  pl.BlockSpec((None, 2, ct), lambda s, c: (s, 0, c)),
                  pl.BlockSpec((None, RNN_CONV_W, ct), lambda s, c: (layer, 0, c)),
                  pl.BlockSpec((None, 1, ct), lambda s, c: (layer, 0, c)),
                  pl.BlockSpec((None, blk, RNN_BLOCK, 4 * RNN_BLOCK), lambda s, c: (layer, c, 0, 0)),
                  pl.BlockSpec((None, blk, 1, 4 * RNN_BLOCK), lambda s, c: (layer, c, 0, 0)),
                  pl.BlockSpec((None, blk, 2, RNN_BLOCK), lambda s, c: (layer, c, 0, 0))] + alias_spec,
        out_specs=[pl.BlockSpec((seq, ct), lambda s, c: (s0 + s, c)),
                   pl.BlockSpec((None, 2, ct), lambda s, c: (s, 0, c))],
        out_shape=[jax.ShapeDtypeStruct((T_ALL, D_RNN), BF),
                   jax.ShapeDtypeStruct((n_seq, 2, D_RNN), F32)],
        scratch_shapes=[pltpu.VMEM((2, seq, ct), F32), pltpu.VMEM((2, seq, ct), F32),
                        pltpu.VMEM((seq, ct), F32), pltpu.VMEM((seq, ct), F32)],
        compiler_params=_params(2),
        name="rg_lru",
        **alias,
    )(z, z, h0, cw, cb, gw, gb, lam, *alias_arg)


def _s5_disc_kernel(are_ref, aim_ref, ldt_ref, bre_ref, bim_ref, abr_ref, abi_ref, bbr_ref, bbi_ref):
    lam_r = jnp.minimum(are_ref[...], -1e-4)
    lam_i = aim_ref[...]
    dt = jnp.exp(ldt_ref[...])
    mag = jnp.exp(lam_r * dt)
    ab_r = mag * jnp.cos(lam_i * dt)
    ab_i = mag * jnp.sin(lam_i * dt)
    den = lam_r * lam_r + lam_i * lam_i
    q_r = ((ab_r - 1.0) * lam_r + ab_i * lam_i) / den
    q_i = (ab_i * lam_r - (ab_r - 1.0) * lam_i) / den
    abr_ref[...] = ab_r
    abi_ref[...] = ab_i
    br = bre_ref[...]
    bi = bim_ref[...]
    bbr_ref[...] = q_r[:, None, :] * br - q_i[:, None, :] * bi
    bbi_ref[...] = q_r[:, None, :] * bi + q_i[:, None, :] * br


def _s5_discretize(a_re, a_im, log_dt, b_re, b_im):
    n = DEPTH * 2 * N_SSM_GROUPS
    bt_re = jnp.swapaxes(b_re, -1, -2).reshape(n, SSM_GROUP, SSM_STATE)
    bt_im = jnp.swapaxes(b_im, -1, -2).reshape(n, SSM_GROUP, SSM_STATE)
    vec = jax.ShapeDtypeStruct((n, SSM_STATE), F32)
    mat = jax.ShapeDtypeStruct((n, SSM_GROUP, SSM_STATE), F32)
    return pl.pallas_call(
        _s5_disc_kernel,
        out_shape=[vec, vec, mat, mat],
        compiler_params=pltpu.CompilerParams(vmem_limit_bytes=VMEM_LIMIT),
        name="s5_discretize",
    )(a_re.reshape(n, SSM_STATE), a_im.reshape(n, SSM_STATE), log_dt.reshape(n, 1), bt_re, bt_im)


def _s5_kernel(u_ref, h0r_ref, h0i_ref, ar_ref, ai_ref, wb_ref, wcr_ref, wci_ref, d_ref, *rest, seq):
    y_ref, fr_ref, fi_ref, lhs_a, lhs_b, acc_s = rest[-6:]
    rows = S5_CHUNK * SUBLANES
    n_chunks = seq // S5_CHUNK

    @pl.when((pl.program_id(0) == 0) & (pl.program_id(1) == 0))
    def _():
        lhs_a[...] = jnp.zeros_like(lhs_a)
        lhs_b[...] = jnp.zeros_like(lhs_b)

    for b in range(S5_SEQS):
        lhs_a[pl.ds(b, seq, stride=SUBLANES), :] = u_ref[b, :, 0:LANES]
        lhs_b[pl.ds(S5_SEQS + b, seq, stride=SUBLANES), :] = u_ref[b, :, LANES:2 * LANES]

    grows = S5_GROUP * SUBLANES
    n_groups = S5_CHUNK // S5_GROUP

    def bu_rows(d, r0, n):
        lhs = jnp.concatenate([lhs_a[pl.ds(r0, n), :], lhs_b[pl.ds(r0, n), :]], axis=1).astype(BF)
        return jnp.dot(lhs, wb_ref[d], preferred_element_type=F32)

    def add_output(d, r0, h_re, h_im):
        y = (jnp.dot(h_re.astype(BF), wcr_ref[d], preferred_element_type=F32)
             - jnp.dot(h_im.astype(BF), wci_ref[d], preferred_element_type=F32))
        rid = lax.broadcasted_iota(jnp.int32, (h_re.shape[0], LANES), 0)
        y = jnp.where((rid & (SUBLANES - 1)) < S5_SEQS, y[:, 0:LANES], y[:, LANES:2 * LANES])
        if d == 0:
            acc_s[pl.ds(r0, h_re.shape[0]), :] = y
        else:
            acc_s[pl.ds(r0, h_re.shape[0]), :] += y

    for d in range(2):
        ar = ar_ref[d]
        ai = ai_ref[d]

        def chunk(c, carry, d=d, ar=ar, ai=ai):
            cc = c if d == 0 else n_chunks - 1 - c
            r0 = pl.multiple_of(cc * rows, rows)
            hr, hi = carry
            order = [g if d == 0 else n_groups - 1 - g for g in range(n_groups)]
            bu_next = bu_rows(d, r0 + order[0] * grows, grows)
            for g in range(n_groups):
                q = r0 + order[g] * grows
                bu = bu_next
                if g + 1 < n_groups:
                    bu_next = bu_rows(d, r0 + order[g + 1] * grows, grows)
                hs_r = [None] * S5_GROUP
                hs_i = [None] * S5_GROUP
                for s in range(S5_GROUP):
                    ss = s if d == 0 else S5_GROUP - 1 - s
                    br = bu[ss * SUBLANES:(ss + 1) * SUBLANES, 0:S5_STATES]
                    bi = bu[ss * SUBLANES:(ss + 1) * SUBLANES, S5_STATES:2 * S5_STATES]
                    hr, hi = ar * hr - ai * hi + br, ar * hi + ai * hr + bi
                    hs_r[ss] = hr
                    hs_i[ss] = hi
                add_output(d, q, jnp.concatenate(hs_r, axis=0), jnp.concatenate(hs_i, axis=0))
            return hr, hi

        hr, hi = lax.fori_loop(0, n_chunks, chunk, (h0r_ref[d], h0i_ref[d]))
        fr_ref[d] = hr
        fi_ref[d] = hi

    for b in range(S5_SEQS):
        y = jnp.concatenate([acc_s[pl.ds(b, seq, stride=SUBLANES), :],
                             acc_s[pl.ds(S5_SEQS + b, seq, stride=SUBLANES), :]], axis=1)
        y_ref[b] = _gelu(y + d_ref[...] * u_ref[b])


def _s5_call(us, h0r, h0i, ab_r, ab_i, wb, wcr, wci, d_skip, layer, *, n_blk, seq, row0, filled=None):
    blk0 = row0 // seq
    st = (None, None, 2, SUBLANES, S5_STATES)
    tok_spec = pl.BlockSpec((S5_SEQS, seq, 2 * LANES), lambda b, p: (0, blk0 + b, p))
    h0_spec = pl.BlockSpec(st, lambda b, p: (0, p, 0, 0, 0))
    par_spec = pl.BlockSpec(st, lambda b, p: (layer, p, 0, 0, 0))
    fin_spec = pl.BlockSpec(st, lambda b, p: (b, p, 0, 0, 0))
    fin_shape = jax.ShapeDtypeStruct((n_blk, S5_PAIRS, 2, SUBLANES, S5_STATES), F32)
    alias = {} if filled is None else dict(input_output_aliases={9: 0})
    alias_spec = [] if filled is None else [pl.BlockSpec(memory_space=pl.ANY)]
    alias_arg = [] if filled is None else [filled]
    return pl.pallas_call(
        functools.partial(_s5_kernel, seq=seq),
        grid=(n_blk, S5_PAIRS),
        in_specs=[tok_spec, h0_spec, h0_spec, par_spec, par_spec,
                  pl.BlockSpec((None, None, 2, 2 * LANES, 2 * S5_STATES), lambda b, p: (layer, p, 0, 0, 0)),
                  pl.BlockSpec((None, None, 2, S5_STATES, 2 * LANES), lambda b, p: (layer, p, 0, 0, 0)),
                  pl.BlockSpec((None, None, 2, S5_STATES, 2 * LANES), lambda b, p: (layer, p, 0, 0, 0)),
                  pl.BlockSpec((None, 1, 2 * LANES), lambda b, p: (layer, 0, p))] + alias_spec,
        out_specs=[tok_spec, fin_spec, fin_spec],
        out_shape=[jax.ShapeDtypeStruct((S5_SEQS, S5_ROWS, D_SSM), F32), fin_shape, fin_shape],
        scratch_shapes=[pltpu.VMEM((seq * SUBLANES, LANES), F32), pltpu.VMEM((seq * SUBLANES, LANES), F32),
                        pltpu.VMEM((seq * SUBLANES, LANES), F32)],
        compiler_params=_params(2),
        name="s5_scan",
        **alias,
    )(us, h0r, h0i, ab_r, ab_i, wb, wcr, wci, d_skip, *alias_arg)


def _s5_pair_rows(st):
    x = st.reshape(S5_SEQS, 2, S5_PAIRS, 2, S5_STATES)
    return x.transpose(2, 1, 3, 0, 4).reshape(1, S5_PAIRS, 2, SUBLANES, S5_STATES)


def _s5_pair_unrows(st, n_seq):
    x = st.reshape(n_seq // S5_SEQS, S5_PAIRS, 2, 2, S5_SEQS, S5_STATES)
    return x.transpose(0, 4, 2, 1, 3, 5).reshape(n_seq, 2, N_SSM_GROUPS, SSM_STATE)


def _glu_kernel(s_ref, w_ref, b_ref, o_ref, wbf):
    @pl.when(pl.program_id(0) == 0)
    def _():
        wbf[...] = w_ref[...].astype(BF)

    ssm = s_ref[...]
    gate = jnp.dot(ssm.astype(BF), wbf[...], preferred_element_type=F32) + b_ref[...]
    o_ref[...] = (ssm * jax.nn.sigmoid(gate)).astype(BF)


def _s5_glu(ssm, glu_w, glu_b, layer):
    def seg_index(i):
        slot, rblk = _segment_block(i)
        return slot, rblk, 0

    vec = pl.BlockSpec((None, 1, D_SSM), lambda i: (layer, 0, 0))
    return pl.pallas_call(
        _glu_kernel,
        grid=(T_ALL // SEQ,),
        in_specs=[pl.BlockSpec((None, SEQ, D_SSM), seg_index),
                  pl.BlockSpec((None, D_SSM, D_SSM), lambda i: (layer, 0, 0)), vec],
        out_specs=pl.BlockSpec((SEQ, D_SSM), lambda i: (i, 0)),
        out_shape=jax.ShapeDtypeStruct((T_ALL, D_SSM), BF),
        scratch_shapes=[pltpu.VMEM((D_SSM, D_SSM), BF)],
        compiler_params=_params(1),
        name="s5_glu",
    )(ssm, glu_w, glu_b.reshape(DEPTH, 1, D_SSM))


def _rope_tables():
    rows = DEC_SEQ // GRID_W
    t_row = jnp.repeat(jnp.arange(rows, dtype=F32), GRID_W)
    t_col = (jnp.arange(DEC_SEQ) % GRID_W).astype(F32)
    inv = 1.0 / (ROPE_THETA ** (jnp.arange(0, AXIS_DIM, 2, dtype=F32) / AXIS_DIM))
    ang_r = t_row[:, None] * inv
    ang_c = t_col[:, None] * inv
    cos = jnp.concatenate([jnp.cos(ang_r), jnp.cos(ang_r), jnp.cos(ang_c), jnp.cos(ang_c)], axis=-1)
    sin = jnp.concatenate([-jnp.sin(ang_r), jnp.sin(ang_r), -jnp.sin(ang_c), jnp.sin(ang_c)], axis=-1)
    return cos, sin


def _block_diag(blocks):
    n, r, c = blocks.shape[-3:]
    eye = jnp.eye(n, dtype=blocks.dtype)
    out = blocks[..., :, :, None, :] * eye[:, None, :, None]
    return out.reshape(blocks.shape[:-3] + (n * r, n * c))


def kernel(x_prompt, x_sample, c, cache_k, cache_v, state_lru, state_s5_re, state_s5_im, c_ctx, w_mod, b_mod, g_mix, w_in, q_gain, k_gain, rnn_conv_w, rnn_conv_b, lru_gate_w, lru_gate_b, lru_lambda, s5_a_re, s5_a_im, s5_log_dt, s5_b_re, s5_b_im, s5_c_re, s5_c_im, s5_d, glu_w, glu_b, w_pa, w_pr, w_ps, w_o, g_ffn, w_up, ffn_conv_w, ffn_conv_b, w_down):
    x = jnp.concatenate([x_prompt.reshape(T_CTX, D_MODEL), x_sample.reshape(T_LAT, D_MODEL)], axis=0)

    cvec = jnp.zeros((MOD_ROWS, D_MODEL), F32).at[0].set(c_ctx).at[1:1 + DEC_BATCH].set(c)
    mod = _modulation(cvec, w_mod, b_mod)
    cos_t, sin_t = _rope_tables()

    gw = lru_gate_w.transpose(0, 3, 4, 1, 2, 5).reshape(DEPTH, N_RNN_BLOCKS, RNN_BLOCK, 4 * RNN_BLOCK).astype(BF)
    gb = (lru_gate_b.reshape(DEPTH, 2, 2, N_RNN_BLOCKS, RNN_BLOCK).transpose(0, 3, 1, 2, 4)
          .reshape(DEPTH, N_RNN_BLOCKS, 1, 4 * RNN_BLOCK))
    lam = lru_lambda.reshape(DEPTH, 2, N_RNN_BLOCKS, RNN_BLOCK).transpose(0, 2, 1, 3)
    cb = rnn_conv_b.reshape(DEPTH, 1, D_RNN)

    ab_r, ab_i, bbt_r, bbt_i = _s5_discretize(s5_a_re, s5_a_im, s5_log_dt, s5_b_re, s5_b_im)
    def pair_rows(ab):
        x = ab.reshape(DEPTH, 2, S5_PAIRS, 2, 1, S5_STATES)
        x = jnp.broadcast_to(x, (DEPTH, 2, S5_PAIRS, 2, S5_SEQS, S5_STATES))
        return x.transpose(0, 2, 1, 3, 4, 5).reshape(DEPTH, S5_PAIRS, 2, SUBLANES, S5_STATES)

    ab_r = pair_rows(ab_r)
    ab_i = pair_rows(ab_i)
    gshape = (DEPTH, 2, S5_TILES, S5_GROUPS, SSM_GROUP, SSM_STATE)
    wb = jnp.concatenate([_block_diag(bbt_r.reshape(gshape)), _block_diag(bbt_i.reshape(gshape))], axis=-1)
    wb = wb.reshape(DEPTH, 2, S5_PAIRS, 2 * LANES, 2 * S5_STATES)
    wb = wb.transpose(0, 2, 1, 3, 4).astype(BF)

    def pair_cols(c):
        x = _block_diag(jnp.swapaxes(c.reshape(gshape), -1, -2))
        x = x.reshape(DEPTH, 2, S5_PAIRS, 2, S5_STATES, LANES)
        return x.transpose(0, 2, 1, 4, 3, 5).reshape(DEPTH, S5_PAIRS, 2, S5_STATES, 2 * LANES).astype(BF)

    wcr = pair_cols(s5_c_re)
    wci = pair_cols(s5_c_im)
    d_skip = s5_d.reshape(DEPTH, 1, D_SSM)

    zeros_lru = jnp.zeros((BATCH, 2, D_RNN), F32)
    zeros_s5 = jnp.zeros((1, S5_PAIRS, 2, SUBLANES, S5_STATES), F32)
    cache_k = cache_k.reshape(DEC_BATCH, DEPTH, PAST_LEN, D_KV)
    cache_v = cache_v.reshape(DEC_BATCH, DEPTH, PAST_LEN, D_KV)

    w_o_bf = w_o.astype(BF)
    w_down_bf = w_down.astype(BF)

    new_k, new_v, new_lru, new_sr, new_si = [], [], [], [], []
    h = _norm_mod(x, g_mix, mod, 0, 0, 1)
    for l in range(DEPTH):
        z, z_gate, us = _in_proj(h, w_in, l)

        attn, k_ctx = _attention(z, q_gain, k_gain, cache_k, cache_v, cos_t, sin_t, l)
        new_k.append(k_ctx.reshape(BATCH, SEQ, N_KV_HEADS, HEAD_DIM))
        new_v.append(z[:T_CTX, OFF_V:OFF_V + D_KV].reshape(BATCH, SEQ, N_KV_HEADS, HEAD_DIM))

        rec, lru_f = _lru_call(z, zeros_lru, rnn_conv_w, cb, gw, gb, lam, l,
                               n_seq=BATCH, seq=SEQ, tok0=0, ct=512)
        rec, _ = _lru_call(z, state_lru[:, l], rnn_conv_w, cb, gw, gb, lam, l,
                           n_seq=DEC_BATCH, seq=DEC_SEQ, tok0=T_CTX, ct=256, filled=rec)
        new_lru.append(lru_f)

        s5, f_r, f_i = _s5_call(us, zeros_s5, zeros_s5, ab_r, ab_i, wb, wcr, wci, d_skip, l,
                                n_blk=BATCH // S5_SEQS, seq=SEQ, row0=0)
        s5, _, _ = _s5_call(us, _s5_pair_rows(state_s5_re[:, l]), _s5_pair_rows(state_s5_im[:, l]),
                            ab_r, ab_i, wb, wcr, wci, d_skip, l,
                            n_blk=DEC_BATCH // S5_SEQS, seq=DEC_SEQ, row0=S5_LAT_ROW0, filled=s5)
        ssm = _s5_glu(s5, glu_w, glu_b, l)
        new_sr.append(_s5_pair_unrows(f_r, BATCH))
        new_si.append(_s5_pair_unrows(f_i, BATCH))

        merged = _merge(attn, rec, ssm, z_gate, w_pa, w_pr, w_ps, l)
        x, h2 = _residual_proj(merged, w_o_bf, x, mod, l, 2, "out_proj", tm=512, norm=(g_ffn, l, 3, 4))

        hmid = _ffn_up(h2, w_up, ffn_conv_w, ffn_conv_b, l)
        if l + 1 < DEPTH:
            x, h = _residual_proj(hmid, w_down_bf, x, mod, l, 5, "ffn_down", tm=256,
                                  norm=(g_mix, l + 1, 0, 1))
        else:
            x, = _residual_proj(hmid, w_down_bf, x, mod, l, 5, "ffn_down", tm=256)

    return (x[:T_CTX].reshape(BATCH, SEQ, D_MODEL), x[T_CTX:].reshape(DEC_BATCH, DEC_SEQ, D_MODEL),
            jnp.stack(new_k, axis=1), jnp.stack(new_v, axis=1), jnp.stack(new_lru, axis=1),
            jnp.stack(new_sr, axis=1), jnp.stack(new_si, axis=1))
```

```python
import functools
import math

import jax
import jax.numpy as jnp
from jax import lax
from jax.experimental import pallas as pl
from jax.experimental.pallas import tpu as pltpu

F32 = jnp.float32
BF = jnp.bfloat16

D_MODEL = 2048
BATCH = 32
SEQ = 256
DEPTH = 4
DEC_BATCH = 4
DEC_SEQ = 1024
PAST_LEN = 256
GRID_W = 64
N_HEADS = 8
N_KV_HEADS = 2
HEAD_DIM = 128
AXIS_DIM = HEAD_DIM // 2
ROPE_THETA = 10000.0
D_ATTN = N_HEADS * HEAD_DIM
D_KV = N_KV_HEADS * HEAD_DIM
D_RNN = D_MODEL // 2
RNN_BLOCK = 128
N_RNN_BLOCKS = D_RNN // RNN_BLOCK
RNN_CONV_W = 4
LRU_C = 8.0
D_SSM = D_MODEL // 2
SSM_GROUP = 16
N_SSM_GROUPS = D_SSM // SSM_GROUP
SSM_STATE = 64
D_FF = 2 * D_MODEL
N_MOD = 6
EPS = 1e-6
D_IN = D_ATTN + 2 * D_KV + 2 * D_RNN + D_SSM + 3 * D_MODEL

OFF_Q = 0
OFF_K = D_ATTN
OFF_V = OFF_K + D_KV
OFF_XR = OFF_V + D_KV
OFF_XG = OFF_XR + D_RNN
OFF_US = OFF_XG + D_RNN
OFF_GL = OFF_US + D_SSM

T_CTX = BATCH * SEQ
T_LAT = DEC_BATCH * DEC_SEQ
T_ALL = T_CTX + T_LAT
MOD_ROWS = 8

SUBLANES = 8
LANES = 128
VMEM_LIMIT = 56 * 1024 * 1024

S5_SEQS = SUBLANES // 2
S5_GROUPS = LANES // SSM_GROUP
S5_TILES = N_SSM_GROUPS // S5_GROUPS
S5_PAIRS = S5_TILES // 2
S5_STATES = S5_GROUPS * SSM_STATE
S5_CHUNK = 128
S5_GROUP = 32
S5_LAT_ROW0 = (BATCH // S5_SEQS) * SEQ
S5_ROWS = S5_LAT_ROW0 + DEC_SEQ


def _params(n_axes):
    return pltpu.CompilerParams(dimension_semantics=("arbitrary",) * n_axes,
                                vmem_limit_bytes=VMEM_LIMIT)


def _mod_row(first_token):
    return jnp.where(first_token < T_CTX, 0, 1 + jnp.maximum(first_token - T_CTX, 0) // DEC_SEQ)


def _gelu(x):
    return jax.nn.gelu(x)


def _sigmoid_tanh(x):
    return 0.5 * jnp.tanh(0.5 * x) + 0.5


def _mod_kernel(c_ref, w_ref, b_ref, o_ref):
    c = c_ref[...]
    s = (c * jax.nn.sigmoid(c)).astype(BF)
    o_ref[...] = jnp.dot(s, w_ref[...].astype(BF), preferred_element_type=F32) + b_ref[...]


def _modulation(cvec, w_mod, b_mod):
    tn = 1536
    n_out = N_MOD * D_MODEL
    return pl.pallas_call(
        _mod_kernel,
        grid=(DEPTH, n_out // tn),
        in_specs=[pl.BlockSpec((MOD_ROWS, D_MODEL), lambda l, j: (0, 0)),
                  pl.BlockSpec((None, D_MODEL, tn), lambda l, j: (l, 0, j)),
                  pl.BlockSpec((None, 1, tn), lambda l, j: (l, 0, j))],
        out_specs=pl.BlockSpec((None, MOD_ROWS, tn), lambda l, j: (l, 0, j)),
        out_shape=jax.ShapeDtypeStruct((DEPTH, MOD_ROWS, n_out), F32),
        compiler_params=_params(2),
        name="modulation",
    )(cvec, w_mod, b_mod.reshape(DEPTH, 1, n_out))


def _norm_kernel(x_ref, g_ref, sc_ref, sh_ref, o_ref, *, tm):
    r = _mod_row(pl.program_id(0) * tm)
    x = x_ref[...]
    y = x * lax.rsqrt(jnp.mean(x * x, axis=-1, keepdims=True) + EPS)
    y = y * g_ref[...]
    sc = sc_ref[pl.ds(r, 1), :]
    sh = sh_ref[pl.ds(r, 1), :]
    o_ref[...] = (y * (1.0 + sc) + sh).astype(BF)


def _norm_mod(x, gain, mod, layer, shift_idx, scale_idx):
    tm = 512
    return pl.pallas_call(
        functools.partial(_norm_kernel, tm=tm),
        grid=(T_ALL // tm,),
        in_specs=[pl.BlockSpec((tm, D_MODEL), lambda i: (i, 0)),
                  pl.BlockSpec((None, 1, D_MODEL), lambda i: (layer, 0, 0)),
                  pl.BlockSpec((None, MOD_ROWS, D_MODEL), lambda i: (layer, 0, scale_idx)),
                  pl.BlockSpec((None, MOD_ROWS, D_MODEL), lambda i: (layer, 0, shift_idx))],
        out_specs=pl.BlockSpec((tm, D_MODEL), lambda i: (i, 0)),
        out_shape=jax.ShapeDtypeStruct((T_ALL, D_MODEL), BF),
        compiler_params=_params(1),
        name="norm_mod",
    )(x, gain.reshape(DEPTH, 1, D_MODEL), mod, mod)


def _ws_kernel(*refs, n_a, n_w, n_extra, n_out, lhs_of, epilogue, tm):
    a_refs = refs[:n_a]
    w_refs = refs[n_a:n_a + n_w]
    extra = refs[n_a + n_w:n_a + n_w + n_extra]
    outs = refs[n_a + n_w + n_extra:n_a + n_w + n_extra + n_out]
    wbf = refs[n_a + n_w + n_extra + n_out:]
    j = pl.program_id(0)
    i = pl.program_id(1)

    @pl.when(i == 0)
    def _():
        for k in range(n_w):
            wbf[k][...] = w_refs[k][...].astype(BF)

    accs = [jnp.dot(a_refs[lhs_of[k]][...], wbf[k][...], preferred_element_type=F32)
            for k in range(n_w)]
    epilogue(accs, extra, outs, j, i * tm)


def _ws_matmul(a_list, w_list, extra, out_shapes, out_specs, epilogue, *, lhs_of, tm, tn, n_tiles, name,
               single_buffer_w=False):
    in_specs, args, scratch = [], [], []
    w_mode = dict(pipeline_mode=pl.Buffered(1)) if single_buffer_w else {}
    for a in a_list:
        in_specs.append(pl.BlockSpec((tm, a.shape[1]), lambda j, i: (i, 0)))
        args.append(a)
    for w, layer, off in w_list:
        kdim = w.shape[1]
        in_specs.append(pl.BlockSpec((None, kdim, tn), lambda j, i, layer=layer, off=off: (layer, 0, off + j),
                                     **w_mode))
        args.append(w)
        scratch.append(pltpu.VMEM((kdim, tn), BF))
    for arr, spec in extra:
        in_specs.append(spec)
        args.append(arr)
    kern = functools.partial(_ws_kernel, n_a=len(a_list), n_w=len(w_list), n_extra=len(extra),
                             n_out=len(out_shapes), lhs_of=lhs_of, epilogue=epilogue, tm=tm)
    return pl.pallas_call(
        kern,
        grid=(n_tiles, T_ALL // tm),
        in_specs=in_specs,
        out_specs=out_specs,
        out_shape=out_shapes,
        scratch_shapes=scratch,
        compiler_params=_params(2),
        name=name,
    )(*args)


def _segment_block(i):
    il = i - BATCH
    per_lat = DEC_SEQ // SEQ
    slot = jnp.where(i < BATCH, i % S5_SEQS, il // per_lat)
    rblk = jnp.where(i < BATCH, i // S5_SEQS, S5_LAT_ROW0 // SEQ + il % per_lat)
    return slot, rblk


def _store_acc(accs, extra, outs, j, row0):
    outs[0][...] = accs[0]


def _in_proj(h, w_in, layer):
    tm = 1024
    tn_a = 896
    z_a = _ws_matmul([h], [(w_in, layer, 0)], [],
                     [jax.ShapeDtypeStruct((T_ALL, OFF_US), F32)],
                     [pl.BlockSpec((tm, tn_a), lambda j, i: (i, j))],
                     _store_acc, lhs_of=(0,), tm=tm, tn=tn_a, n_tiles=OFF_US // tn_a, name="in_proj_a")[0]
    tn_g = 1536

    def gate_epilogue(accs, extra, outs, j, row0):
        outs[0][...] = jax.nn.sigmoid(accs[0]).astype(BF)

    z_g = _ws_matmul([h], [(w_in, layer, OFF_GL // tn_g)], [],
                     [jax.ShapeDtypeStruct((T_ALL, 3 * D_MODEL), BF)],
                     [pl.BlockSpec((tm, tn_g), lambda j, i: (i, j))],
                     gate_epilogue, lhs_of=(0,), tm=tm, tn=tn_g, n_tiles=3 * D_MODEL // tn_g,
                     name="in_proj_g", single_buffer_w=True)[0]

    tn_u = 512

    def us_index(j, i):
        slot, rblk = _segment_block(i)
        return slot, rblk, 0

    def us_epilogue(accs, extra, outs, j, row0):
        outs[0][...] = jnp.concatenate(accs, axis=1)

    us = _ws_matmul([h], [(w_in, layer, OFF_US // tn_u), (w_in, layer, OFF_US // tn_u + 1)], [],
                    [jax.ShapeDtypeStruct((S5_SEQS, S5_ROWS, D_SSM), F32)],
                    [pl.BlockSpec((None, SEQ, D_SSM), us_index)],
                    us_epilogue, lhs_of=(0, 0), tm=SEQ, tn=tn_u, n_tiles=1, name="in_proj_u")[0]
    return z_a, z_g, us


def _merge_out_kernel(attn_ref, rec_ref, ssm_ref, g_ref, wpa_ref, wpr_ref, wps_ref, wo_ref, x_ref, gate_ref,
                      gain_ref, sc_ref, sh_ref, x_out, h_out, *, tm):
    r = _mod_row(pl.program_id(0) * tm)
    m = g_ref[:, 0:D_MODEL].astype(F32) * jnp.dot(attn_ref[...], wpa_ref[...], preferred_element_type=F32)
    m = m + (g_ref[:, D_MODEL:2 * D_MODEL].astype(F32)
             * jnp.dot(rec_ref[...], wpr_ref[...], preferred_element_type=F32))
    m = m + (g_ref[:, 2 * D_MODEL:3 * D_MODEL].astype(F32)
             * jnp.dot(ssm_ref[...], wps_ref[...], preferred_element_type=F32))
    acc = jnp.dot(m.astype(BF), wo_ref[...], preferred_element_type=F32)
    x = x_ref[...] + gate_ref[pl.ds(r, 1), :] * acc
    x_out[...] = x
    y = x * lax.rsqrt(jnp.mean(x * x, axis=-1, keepdims=True) + EPS)
    y = y * gain_ref[...]
    h_out[...] = (y * (1.0 + sc_ref[pl.ds(r, 1), :]) + sh_ref[pl.ds(r, 1), :]).astype(BF)


def _merge_out(attn, rec, ssm, gates, w_pa, w_pr, w_ps, w_o, x, mod, g_ffn, layer):
    tm = 256
    row = pl.BlockSpec((tm, D_MODEL), lambda i: (i, 0))
    branch = pl.BlockSpec((tm, D_ATTN), lambda i: (i, 0))
    resident = dict(pipeline_mode=pl.Buffered(1))
    wbr = pl.BlockSpec((None, D_ATTN, D_MODEL), lambda i: (layer, 0, 0), **resident)

    def mod_spec(idx):
        return pl.BlockSpec((None, MOD_ROWS, D_MODEL), lambda i: (layer, 0, idx))

    return pl.pallas_call(
        functools.partial(_merge_out_kernel, tm=tm),
        grid=(T_ALL // tm,),
        in_specs=[branch, branch, branch,
                  pl.BlockSpec((tm, 3 * D_MODEL), lambda i: (i, 0)),
                  wbr, wbr, wbr,
                  pl.BlockSpec((None, D_MODEL, D_MODEL), lambda i: (layer, 0, 0), **resident),
                  row, mod_spec(2),
                  pl.BlockSpec((None, 1, D_MODEL), lambda i: (layer, 0, 0)), mod_spec(4), mod_spec(3)],
        out_specs=[row, row],
        out_shape=[jax.ShapeDtypeStruct((T_ALL, D_MODEL), F32), jax.ShapeDtypeStruct((T_ALL, D_MODEL), BF)],
        compiler_params=_params(1),
        name="merge_out",
    )(attn, rec, ssm, gates, w_pa, w_pr, w_ps, w_o, x, mod, g_ffn.reshape(DEPTH, 1, D_MODEL), mod, mod)


def _residual_kernel(a_ref, w_ref, x_ref, gate_ref, *rest, tm, with_norm):
    r = _mod_row(pl.program_id(0) * tm)
    acc = jnp.dot(a_ref[...], w_ref[...], preferred_element_type=F32)
    x = x_ref[...] + gate_ref[pl.ds(r, 1), :] * acc
    if not with_norm:
        rest[0][...] = x
        return
    g_ref, sc_ref, sh_ref, x_out, h_out = rest
    x_out[...] = x
    y = x * lax.rsqrt(jnp.mean(x * x, axis=-1, keepdims=True) + EPS)
    y = y * g_ref[...]
    h_out[...] = (y * (1.0 + sc_ref[pl.ds(r, 1), :]) + sh_ref[pl.ds(r, 1), :]).astype(BF)


def _residual_proj(a, w_bf, x, mod, layer, gate_idx, name, *, tm, norm=None):
    kdim = a.shape[1]
    row = pl.BlockSpec((tm, D_MODEL), lambda i: (i, 0))
    in_specs = [pl.BlockSpec((tm, kdim), lambda i: (i, 0)),
                pl.BlockSpec((None, kdim, D_MODEL), lambda i: (layer, 0, 0), pipeline_mode=pl.Buffered(1)),
                row,
                pl.BlockSpec((None, MOD_ROWS, D_MODEL), lambda i: (layer, 0, gate_idx))]
    args = [a, w_bf, x, mod]
    out_shape = [jax.ShapeDtypeStruct((T_ALL, D_MODEL), F32)]
    out_specs = [row]
    if norm is not None:
        gain, nl, shift_idx, scale_idx = norm
        in_specs += [pl.BlockSpec((None, 1, D_MODEL), lambda i: (nl, 0, 0)),
                     pl.BlockSpec((None, MOD_ROWS, D_MODEL), lambda i: (nl, 0, scale_idx)),
                     pl.BlockSpec((None, MOD_ROWS, D_MODEL), lambda i: (nl, 0, shift_idx))]
        args += [gain.reshape(DEPTH, 1, D_MODEL), mod, mod]
        out_shape.append(jax.ShapeDtypeStruct((T_ALL, D_MODEL), BF))
        out_specs.append(row)
    return pl.pallas_call(
        functools.partial(_residual_kernel, tm=tm, with_norm=norm is not None),
        grid=(T_ALL // tm,),
        in_specs=in_specs,
        out_specs=out_specs,
        out_shape=out_shape,
        compiler_params=_params(1),
        name=name,
    )(*args)


def _ffn_up(h2, w_up, conv_w, conv_b, layer):
    tm, tn = DEC_SEQ, 512

    def epilogue(accs, extra, outs, j, row0):
        a, b = accs
        seq_len = jnp.where(row0 < T_CTX, SEQ, DEC_SEQ)
        pos = lax.broadcasted_iota(jnp.int32, a.shape, 0) & (seq_len - 1)
        prev = jnp.where(pos == 0, 0.0, pltpu.roll(a, 1, 0))
        nxt = jnp.where(pos == seq_len - 1, 0.0, pltpu.roll(a, tm - 1, 0))
        cw = extra[0][...]
        a = cw[0:1] * prev + cw[1:2] * a + cw[2:3] * nxt + extra[1][...]
        outs[0][...] = (_gelu(a) * b).astype(BF)

    extra = [(conv_w, pl.BlockSpec((None, 3, tn), lambda j, i: (layer, 0, j))),
             (conv_b.reshape(DEPTH, 1, D_FF), pl.BlockSpec((None, 1, tn), lambda j, i: (layer, 0, j)))]
    return _ws_matmul([h2], [(w_up, layer, 0), (w_up, layer, D_FF // tn)], extra,
                      [jax.ShapeDtypeStruct((T_ALL, D_FF), BF)],
                      [pl.BlockSpec((tm, tn), lambda j, i: (i, j))],
                      epilogue, lhs_of=(0, 0), tm=tm, tn=tn, n_tiles=D_FF // tn, name="ffn_up")[0]


def _head_norm(x, gain):
    return x * lax.rsqrt(jnp.mean(x * x, axis=-1, keepdims=True) + EPS) * gain


def _rope(x, cos, sin_signed):
    lane = lax.broadcasted_iota(jnp.int32, x.shape, 1)
    first_half = (lane & (AXIS_DIM - 1)) < AXIS_DIM // 2
    partner = jnp.where(first_half, pltpu.roll(x, HEAD_DIM - AXIS_DIM // 2, 1),
                        pltpu.roll(x, AXIS_DIM // 2, 1))
    return x * cos + partner * sin_signed


def _scores(q, k):
    return lax.dot_general(q, k, (((1,), (1,)), ((), ())), preferred_element_type=F32) * (HEAD_DIM ** -0.5)


def _attn_ctx_kernel(q_ref, k_ref, v_ref, qg_ref, kg_ref, keep_a, keep_k, keep_v, o_ref, ko_ref, vo_ref):
    del keep_a, keep_k, keep_v
    grp = N_HEADS // N_KV_HEADS
    vo_ref[...] = v_ref[...]
    for kv in range(N_KV_HEADS):
        ks = slice(kv * HEAD_DIM, (kv + 1) * HEAD_DIM)
        kn = _head_norm(k_ref[:, ks], kg_ref[...])
        ko_ref[:, ks] = kn
        kn = kn.astype(BF)
        vv = v_ref[:, ks].astype(BF)
        for g in range(grp):
            hs = slice((kv * grp + g) * HEAD_DIM, (kv * grp + g + 1) * HEAD_DIM)
            qn = _head_norm(q_ref[:, hs], qg_ref[...]).astype(BF)
            s = _scores(qn, kn)
            p = jnp.exp(s - jnp.max(s, axis=-1, keepdims=True))
            den = jnp.sum(p, axis=-1, keepdims=True)
            o = jnp.dot(p.astype(BF), vv, preferred_element_type=F32) / den
            o_ref[:, hs] = o.astype(BF)


def _attn_lat_kernel(q_ref, k_ref, v_ref, ck_ref, cv_ref, cosq_ref, sinq_ref, cosk_ref, sink_ref,
                     qg_ref, kg_ref, filled_ref, o_ref):
    del filled_ref
    grp = N_HEADS // N_KV_HEADS
    for kv in range(N_KV_HEADS):
        ks = slice(kv * HEAD_DIM, (kv + 1) * HEAD_DIM)
        kn = _rope(_head_norm(k_ref[:, ks], kg_ref[...]), cosk_ref[...], sink_ref[...]).astype(BF)
        vv = v_ref[:, ks].astype(BF)
        ck = ck_ref[:, ks].astype(BF)
        cv = cv_ref[:, ks].astype(BF)
        for g in range(grp):
            hs = slice((kv * grp + g) * HEAD_DIM, (kv * grp + g + 1) * HEAD_DIM)
            qn = _rope(_head_norm(q_ref[:, hs], qg_ref[...]), cosq_ref[...], sinq_ref[...]).astype(BF)
            s1 = _scores(qn, ck)
            s2 = _scores(qn, kn)
            m = jnp.maximum(jnp.max(s1, axis=-1, keepdims=True), jnp.max(s2, axis=-1, keepdims=True))
            p1 = jnp.exp(s1 - m)
            p2 = jnp.exp(s2 - m)
            den = jnp.sum(p1, axis=-1, keepdims=True) + jnp.sum(p2, axis=-1, keepdims=True)
            o = (jnp.dot(p1.astype(BF), cv, preferred_element_type=F32)
                 + jnp.dot(p2.astype(BF), vv, preferred_element_type=F32)) / den
            o_ref[:, hs] = o.astype(BF)


def _attention(z, q_gain, k_gain, cache_k, cache_v, cos_t, sin_t, layer, attn_buf, new_k, new_v):
    gains = [pl.BlockSpec((None, 1, HEAD_DIM), lambda *_: (layer, 0, 0))] * 2
    qg = q_gain.reshape(DEPTH, 1, HEAD_DIM)
    kg = k_gain.reshape(DEPTH, 1, HEAD_DIM)
    kcol, vcol = OFF_K // D_KV, OFF_V // D_KV
    keep = pl.BlockSpec(memory_space=pl.ANY)
    cache_spec = pl.BlockSpec((None, None, SEQ, D_KV), lambda b: (b, layer, 0, 0))
    attn, new_k, new_v = pl.pallas_call(
        _attn_ctx_kernel,
        grid=(BATCH,),
        in_specs=[pl.BlockSpec((SEQ, D_ATTN), lambda b: (b, 0)),
                  pl.BlockSpec((SEQ, D_KV), lambda b: (b, kcol)),
                  pl.BlockSpec((SEQ, D_KV), lambda b: (b, vcol))] + gains + [keep, keep, keep],
        out_specs=[pl.BlockSpec((SEQ, D_ATTN), lambda b: (b, 0)), cache_spec, cache_spec],
        out_shape=[jax.ShapeDtypeStruct((T_ALL, D_ATTN), BF),
                   jax.ShapeDtypeStruct((BATCH, DEPTH, SEQ, D_KV), F32),
                   jax.ShapeDtypeStruct((BATCH, DEPTH, SEQ, D_KV), F32)],
        input_output_aliases={5: 0, 6: 1, 7: 2},
        compiler_params=_params(1),
        name="attn_ctx",
    )(z, z, z, qg, kg, attn_buf, new_k, new_v)

    tq = 512
    nq = DEC_SEQ // tq
    lat0 = T_CTX // DEC_SEQ
    attn = pl.pallas_call(
        _attn_lat_kernel,
        grid=(DEC_BATCH, nq),
        in_specs=[pl.BlockSpec((tq, D_ATTN), lambda b, i: ((lat0 + b) * nq + i, 0)),
                  pl.BlockSpec((DEC_SEQ, D_KV), lambda b, i: (lat0 + b, kcol)),
                  pl.BlockSpec((DEC_SEQ, D_KV), lambda b, i: (lat0 + b, vcol)),
                  pl.BlockSpec((None, None, PAST_LEN, D_KV), lambda b, i: (b, layer, 0, 0)),
                  pl.BlockSpec((None, None, PAST_LEN, D_KV), lambda b, i: (b, layer, 0, 0)),
                  pl.BlockSpec((tq, HEAD_DIM), lambda b, i: (i, 0)),
                  pl.BlockSpec((tq, HEAD_DIM), lambda b, i: (i, 0)),
                  pl.BlockSpec((DEC_SEQ, HEAD_DIM), lambda b, i: (0, 0)),
                  pl.BlockSpec((DEC_SEQ, HEAD_DIM), lambda b, i: (0, 0))] + gains
                 + [pl.BlockSpec(memory_space=pl.ANY)],
        out_specs=pl.BlockSpec((tq, D_ATTN), lambda b, i: ((lat0 + b) * nq + i, 0)),
        out_shape=jax.ShapeDtypeStruct((T_ALL, D_ATTN), BF),
        input_output_aliases={11: 0},
        compiler_params=_params(2),
        name="attn_lat",
    )(z, z, z, cache_k, cache_v, cos_t, sin_t, cos_t, sin_t, qg, kg, attn)
    return attn, new_k, new_v


def _lru_kernel(xr_ref, xg_ref, h0_ref, cw_ref, cb_ref, gw_ref, gb_ref, lam_ref, *rest, seq, ct):
    o_ref, hf_ref, a_s, b_s, hfw_s, hbw_s = rest[-6:]
    n_blocks = ct // RNN_BLOCK
    n_tiles = seq // SUBLANES
    row = lax.broadcasted_iota(jnp.int32, (seq, ct), 0)
    xr = xr_ref[...]

    def shifted(x, k):
        if k > 0:
            return jnp.where(row >= k, pltpu.roll(x, k, 0), 0.0)
        return jnp.where(row < seq + k, pltpu.roll(x, seq + k, 0), 0.0)

    cw = cw_ref[...]
    xc = (cb_ref[...] + cw[0:1] * shifted(xr, 2) + cw[1:2] * shifted(xr, 1) + cw[2:3] * xr
          + cw[3:4] * shifted(xr, -1))

    for kb in range(n_blocks):
        cs = slice(kb * RNN_BLOCK, (kb + 1) * RNN_BLOCK)
        xcb = xc[:, cs]
        gates = jnp.dot(xcb.astype(BF), gw_ref[kb], preferred_element_type=F32) + gb_ref[kb]
        for d in range(2):
            a, b = _lru_gate_terms(xcb, gates, lam_ref, kb, d)
            a_s[d, :, cs] = a
            b_s[d, :, cs] = b

    sub = lax.broadcasted_iota(jnp.int32, (SUBLANES, ct), 0)

    def tile_scan(a, b, forward):
        for k in (1, 2, 4):
            if forward:
                keep = sub >= k
                shift = k
            else:
                keep = sub < SUBLANES - k
                shift = SUBLANES - k
            a_n = jnp.where(keep, pltpu.roll(a, shift, 0), 1.0)
            b_n = jnp.where(keep, pltpu.roll(b, shift, 0), 0.0)
            b = a * b_n + b
            a = a * a_n
        return a, b

    def body(j, carry):
        hf, hb = carry
        r0 = pl.multiple_of(j * SUBLANES, SUBLANES)
        a, b = tile_scan(a_s[0, pl.ds(r0, SUBLANES), :], b_s[0, pl.ds(r0, SUBLANES), :], True)
        h = a * hf + b
        hfw_s[pl.ds(r0, SUBLANES), :] = h
        hf = h[SUBLANES - 1:SUBLANES, :]
        r1 = pl.multiple_of((n_tiles - 1 - j) * SUBLANES, SUBLANES)
        a, b = tile_scan(a_s[1, pl.ds(r1, SUBLANES), :], b_s[1, pl.ds(r1, SUBLANES), :], False)
        h = a * hb + b
        hbw_s[pl.ds(r1, SUBLANES), :] = h
        hb = h[0:1, :]
        return hf, hb

    hf, hb = lax.fori_loop(0, n_tiles, body, (h0_ref[0:1, :], h0_ref[1:2, :]))
    hf_ref[0:1, :] = hf
    hf_ref[1:2, :] = hb
    o_ref[...] = (_gelu(xg_ref[...]) * (hfw_s[...] + hbw_s[...])).astype(BF)


def _lru_call(z, h0, cw, cb, gw, gb, lam, layer, *, n_seq, seq, tok0, ct, filled=None):
    nc = D_RNN // ct
    s0 = tok0 // seq
    blk = ct // RNN_BLOCK
    alias = {} if filled is None else dict(input_output_aliases={8: 0})
    alias_spec = [] if filled is None else [pl.BlockSpec(memory_space=pl.ANY)]
    alias_arg = [] if filled is None else [filled]
    return pl.pallas_call(
        functools.partial(_lru_kernel, seq=seq, ct=ct),
        grid=(n_seq, nc),
        in_specs=[pl.BlockSpec((seq, ct), lambda s, c: (s0 + s, OFF_XR // ct + c)),
                  pl.BlockSpec((seq, ct), lambda s, c: (s0 + s, OFF_XG // ct + c)),
                  pl.BlockSpec((None, 2, ct), lambda s, c: (s, 0, c)),
                  pl.BlockSpec((None, RNN_CONV_W, ct), lambda s, c: (layer, 0, c)),
                  pl.BlockSpec((None, 1, ct), lambda s, c: (layer, 0, c)),
                  pl.BlockSpec((None, blk, RNN_BLOCK, 4 * RNN_BLOCK), lambda s, c: (layer, c, 0, 0)),
                  pl.BlockSpec((None, blk, 1, 4 * RNN_BLOCK), lambda s, c: (layer, c, 0, 0)),
                  pl.BlockSpec((None, blk, 2, RNN_BLOCK), lambda s, c: (layer, c, 0, 0))] + alias_spec,
        out_specs=[pl.BlockSpec((seq, ct), lambda s, c: (s0 + s, c)),
                   pl.BlockSpec((None, 2, ct), lambda s, c: (s, 0, c))],
        out_shape=[jax.ShapeDtypeStruct((T_ALL, D_RNN), BF),
                   jax.ShapeDtypeStruct((n_seq, 2, D_RNN), F32)],
        scratch_shapes=[pltpu.VMEM((2, seq, ct), F32), pltpu.VMEM((2, seq, ct), F32),
                        pltpu.VMEM((seq, ct), F32), pltpu.VMEM((seq, ct), F32)],
        compiler_params=_params(2),
        name="rg_lru",
        **alias,
    )(z, z, h0, cw, cb, gw, gb, lam, *alias_arg)


def _lru_gate_terms(xc, gates, lam_ref, kb, d):
    base = d * 2 * RNN_BLOCK
    r = _sigmoid_tanh(gates[:, base:base + RNN_BLOCK])
    ig = _sigmoid_tanh(gates[:, base + RNN_BLOCK:base + 2 * RNN_BLOCK])
    lam = lam_ref[kb, d:d + 1, :]
    a = jnp.exp(r * (-LRU_C * jnp.log1p(jnp.exp(-lam))))
    om = 1.0 - a * a
    root = jnp.where(om > 0.0, om * lax.rsqrt(om), 0.0)
    return a, root * (ig * xc)


LRU_CTX_SEQS = SUBLANES
LRU_PAD_FRONT = (RNN_CONV_W // 2) * SUBLANES
LRU_PAD_BACK = (RNN_CONV_W - 1 - RNN_CONV_W // 2) * SUBLANES


def _lru_ctx_kernel(xr_ref, xg_ref, cw_ref, cb_ref, gw_ref, gb_ref, lam_ref, keep_ref, o_ref, hf_ref,
                    xs, a_s, b_s, hfw, hbw, *, ct):
    del keep_ref
    n_blocks = ct // RNN_BLOCK
    rows = SEQ * SUBLANES

    @pl.when((pl.program_id(0) == 0) & (pl.program_id(1) == 0))
    def _():
        xs[...] = jnp.zeros_like(xs)

    for kb in range(n_blocks):
        cs = slice(kb * RNN_BLOCK, (kb + 1) * RNN_BLOCK)
        for b in range(LRU_CTX_SEQS):
            xs[kb, pl.ds(LRU_PAD_FRONT + b, SEQ, stride=SUBLANES), :] = xr_ref[b, :, cs]

    for kb in range(n_blocks):
        cs = slice(kb * RNN_BLOCK, (kb + 1) * RNN_BLOCK)
        xc = cb_ref[:, cs]
        for j in range(RNN_CONV_W):
            xc = xc + cw_ref[j:j + 1, cs] * xs[kb, j * SUBLANES:j * SUBLANES + rows, :]
        gates = jnp.dot(xc.astype(BF), gw_ref[kb], preferred_element_type=F32) + gb_ref[kb]
        for d in range(2):
            a, b = _lru_gate_terms(xc, gates, lam_ref, kb, d)
            a_s[d, kb] = a
            b_s[d, kb] = b

    def body(t, carry):
        hf, hb = carry
        r0 = pl.multiple_of(t * SUBLANES, SUBLANES)
        r1 = pl.multiple_of((SEQ - 1 - t) * SUBLANES, SUBLANES)
        new_f, new_b = [], []
        for kb in range(n_blocks):
            f = a_s[0, kb, pl.ds(r0, SUBLANES), :] * hf[kb] + b_s[0, kb, pl.ds(r0, SUBLANES), :]
            hfw[kb, pl.ds(r0, SUBLANES), :] = f
            new_f.append(f)
            g = a_s[1, kb, pl.ds(r1, SUBLANES), :] * hb[kb] + b_s[1, kb, pl.ds(r1, SUBLANES), :]
            hbw[kb, pl.ds(r1, SUBLANES), :] = g
            new_b.append(g)
        return tuple(new_f), tuple(new_b)

    zero = tuple(jnp.zeros((SUBLANES, RNN_BLOCK), F32) for _ in range(n_blocks))
    hf, hb = lax.fori_loop(0, SEQ, body, (zero, zero), unroll=8)
    for kb in range(n_blocks):
        cs = slice(kb * RNN_BLOCK, (kb + 1) * RNN_BLOCK)
        hf_ref[0, :, cs] = hf[kb]
        hf_ref[1, :, cs] = hb[kb]
        for b in range(LRU_CTX_SEQS):
            h = (hfw[kb, pl.ds(b, SEQ, stride=SUBLANES), :] + hbw[kb, pl.ds(b, SEQ, stride=SUBLANES), :])
            o_ref[b, :, cs] = (_gelu(xg_ref[b, :, cs]) * h).astype(BF)


def _lru_ctx_call(z, cw, cb, gw, gb, lam, layer, filled):
    ct = 256
    blk = ct // RNN_BLOCK
    n_seg = T_ALL // SEQ
    z3 = z.reshape(n_seg, SEQ, z.shape[1])
    rows = SEQ * SUBLANES
    pad_rows = LRU_PAD_FRONT + rows + LRU_PAD_BACK
    tok = pl.BlockSpec((LRU_CTX_SEQS, SEQ, ct), lambda s, c: (s, 0, c))
    rec, fin = pl.pallas_call(
        functools.partial(_lru_ctx_kernel, ct=ct),
        grid=(BATCH // LRU_CTX_SEQS, D_RNN // ct),
        in_specs=[pl.BlockSpec((LRU_CTX_SEQS, SEQ, ct), lambda s, c: (s, 0, OFF_XR // ct + c)),
                  pl.BlockSpec((LRU_CTX_SEQS, SEQ, ct), lambda s, c: (s, 0, OFF_XG // ct + c)),
                  pl.BlockSpec((None, RNN_CONV_W, ct), lambda s, c: (layer, 0, c)),
                  pl.BlockSpec((None, 1, ct), lambda s, c: (layer, 0, c)),
                  pl.BlockSpec((None, blk, RNN_BLOCK, 4 * RNN_BLOCK), lambda s, c: (layer, c, 0, 0)),
                  pl.BlockSpec((None, blk, 1, 4 * RNN_BLOCK), lambda s, c: (layer, c, 0, 0)),
                  pl.BlockSpec((None, blk, 2, RNN_BLOCK), lambda s, c: (layer, c, 0, 0)),
                  pl.BlockSpec(memory_space=pl.ANY)],
        out_specs=[tok, pl.BlockSpec((None, 2, LRU_CTX_SEQS, ct), lambda s, c: (s, 0, 0, c))],
        out_shape=[jax.ShapeDtypeStruct((n_seg, SEQ, D_RNN), BF),
                   jax.ShapeDtypeStruct((BATCH // LRU_CTX_SEQS, 2, LRU_CTX_SEQS, D_RNN), F32)],
        scratch_shapes=[pltpu.VMEM((blk, pad_rows, RNN_BLOCK), F32),
                        pltpu.VMEM((2, blk, rows, RNN_BLOCK), F32), pltpu.VMEM((2, blk, rows, RNN_BLOCK), F32),
                        pltpu.VMEM((blk, rows, RNN_BLOCK), F32), pltpu.VMEM((blk, rows, RNN_BLOCK), F32)],
        input_output_aliases={7: 0},
        compiler_params=_params(2),
        name="rg_lru_ctx",
    )(z3, z3, cw, cb, gw, gb, lam, filled.reshape(n_seg, SEQ, D_RNN))
    return rec.reshape(T_ALL, D_RNN), fin.transpose(0, 2, 1, 3).reshape(BATCH, 2, D_RNN)


def _s5_disc_kernel(are_ref, aim_ref, ldt_ref, bre_ref, bim_ref, abr_ref, abi_ref, bbr_ref, bbi_ref):
    lam_r = jnp.minimum(are_ref[...], -1e-4)
    lam_i = aim_ref[...]
    dt = jnp.exp(ldt_ref[...])
    mag = jnp.exp(lam_r * dt)
    ab_r = mag * jnp.cos(lam_i * dt)
    ab_i = mag * jnp.sin(lam_i * dt)
    den = lam_r * lam_r + lam_i * lam_i
    q_r = ((ab_r - 1.0) * lam_r + ab_i * lam_i) / den
    q_i = (ab_i * lam_r - (ab_r - 1.0) * lam_i) / den
    abr_ref[...] = ab_r
    abi_ref[...] = ab_i
    br = bre_ref[...]
    bi = bim_ref[...]
    bbr_ref[...] = q_r[:, None, :] * br - q_i[:, None, :] * bi
    bbi_ref[...] = q_r[:, None, :] * bi + q_i[:, None, :] * br


def _s5_discretize(a_re, a_im, log_dt, b_re, b_im):
    n = DEPTH * 2 * N_SSM_GROUPS
    bt_re = jnp.swapaxes(b_re, -1, -2).reshape(n, SSM_GROUP, SSM_STATE)
    bt_im = jnp.swapaxes(b_im, -1, -2).reshape(n, SSM_GROUP, SSM_STATE)
    vec = jax.ShapeDtypeStruct((n, SSM_STATE), F32)
    mat = jax.ShapeDtypeStruct((n, SSM_GROUP, SSM_STATE), F32)
    return pl.pallas_call(
        _s5_disc_kernel,
        out_shape=[vec, vec, mat, mat],
        compiler_params=pltpu.CompilerParams(vmem_limit_bytes=VMEM_LIMIT),
        name="s5_discretize",
    )(a_re.reshape(n, SSM_STATE), a_im.reshape(n, SSM_STATE), log_dt.reshape(n, 1), bt_re, bt_im)


def _s5_kernel(u_ref, h0r_ref, h0i_ref, ar_ref, ai_ref, wb_ref, wcr_ref, wci_ref, d_ref, *rest, seq):
    y_ref, fr_ref, fi_ref, lhs_a, lhs_b, acc_s = rest[-6:]
    rows = S5_CHUNK * SUBLANES
    n_chunks = seq // S5_CHUNK

    @pl.when((pl.program_id(0) == 0) & (pl.program_id(1) == 0))
    def _():
        lhs_a[...] = jnp.zeros_like(lhs_a)
        lhs_b[...] = jnp.zeros_like(lhs_b)

    for b in range(S5_SEQS):
        lhs_a[pl.ds(b, seq, stride=SUBLANES), :] = u_ref[b, :, 0:LANES]
        lhs_b[pl.ds(S5_SEQS + b, seq, stride=SUBLANES), :] = u_ref[b, :, LANES:2 * LANES]

    grows = S5_GROUP * SUBLANES
    n_groups = S5_CHUNK // S5_GROUP

    def bu_rows(d, r0, n):
        lhs = jnp.concatenate([lhs_a[pl.ds(r0, n), :], lhs_b[pl.ds(r0, n), :]], axis=1).astype(BF)
        return jnp.dot(lhs, wb_ref[d], preferred_element_type=F32)

    def add_output(d, r0, h_re, h_im):
        y = (jnp.dot(h_re.astype(BF), wcr_ref[d], preferred_element_type=F32)
             - jnp.dot(h_im.astype(BF), wci_ref[d], preferred_element_type=F32))
        rid = lax.broadcasted_iota(jnp.int32, (h_re.shape[0], LANES), 0)
        y = jnp.where((rid & (SUBLANES - 1)) < S5_SEQS, y[:, 0:LANES], y[:, LANES:2 * LANES])
        if d == 0:
            acc_s[pl.ds(r0, h_re.shape[0]), :] = y
        else:
            acc_s[pl.ds(r0, h_re.shape[0]), :] += y

    for d in range(2):
        ar = ar_ref[d]
        ai = ai_ref[d]

        def chunk(c, carry, d=d, ar=ar, ai=ai):
            cc = c if d == 0 else n_chunks - 1 - c
            r0 = pl.multiple_of(cc * rows, rows)
            hr, hi = carry
            order = [g if d == 0 else n_groups - 1 - g for g in range(n_groups)]
            bu_next = bu_rows(d, r0 + order[0] * grows, grows)
            for g in range(n_groups):
                q = r0 + order[g] * grows
                bu = bu_next
                if g + 1 < n_groups:
                    bu_next = bu_rows(d, r0 + order[g + 1] * grows, grows)
                hs_r = [None] * S5_GROUP
                hs_i = [None] * S5_GROUP
                for s in range(S5_GROUP):
                    ss = s if d == 0 else S5_GROUP - 1 - s
                    br = bu[ss * SUBLANES:(ss + 1) * SUBLANES, 0:S5_STATES]
                    bi = bu[ss * SUBLANES:(ss + 1) * SUBLANES, S5_STATES:2 * S5_STATES]
                    hr, hi = ar * hr - ai * hi + br, ar * hi + ai * hr + bi
                    hs_r[ss] = hr
                    hs_i[ss] = hi
                add_output(d, q, jnp.concatenate(hs_r, axis=0), jnp.concatenate(hs_i, axis=0))
            return hr, hi

        hr, hi = lax.fori_loop(0, n_chunks, chunk, (h0r_ref[d], h0i_ref[d]))
        fr_ref[d] = hr
        fi_ref[d] = hi

    for b in range(S5_SEQS):
        y = jnp.concatenate([acc_s[pl.ds(b, seq, stride=SUBLANES), :],
                             acc_s[pl.ds(S5_SEQS + b, seq, stride=SUBLANES), :]], axis=1)
        y_ref[b] = _gelu(y + d_ref[...] * u_ref[b])


def _s5_call(us, h0r, h0i, h0_idx, ab_r, ab_i, wb, wcr, wci, d_skip, layer, *, n_blk, seq, row0, filled=None):
    blk0 = row0 // seq
    st = (None, None, 2, SUBLANES, S5_STATES)
    tok_spec = pl.BlockSpec((S5_SEQS, seq, 2 * LANES), lambda b, p: (0, blk0 + b, p))
    h0_spec = pl.BlockSpec(st, lambda b, p: (h0_idx, p, 0, 0, 0))
    par_spec = pl.BlockSpec(st, lambda b, p: (layer, p, 0, 0, 0))
    fin_spec = pl.BlockSpec(st, lambda b, p: (b, p, 0, 0, 0))
    fin_shape = jax.ShapeDtypeStruct((n_blk, S5_PAIRS, 2, SUBLANES, S5_STATES), F32)
    alias = {} if filled is None else dict(input_output_aliases={9: 0})
    alias_spec = [] if filled is None else [pl.BlockSpec(memory_space=pl.ANY)]
    alias_arg = [] if filled is None else [filled]
    return pl.pallas_call(
        functools.partial(_s5_kernel, seq=seq),
        grid=(n_blk, S5_PAIRS),
        in_specs=[tok_spec, h0_spec, h0_spec, par_spec, par_spec,
                  pl.BlockSpec((None, None, 2, 2 * LANES, 2 * S5_STATES), lambda b, p: (layer, p, 0, 0, 0)),
                  pl.BlockSpec((None, None, 2, S5_STATES, 2 * LANES), lambda b, p: (layer, p, 0, 0, 0)),
                  pl.BlockSpec((None, None, 2, S5_STATES, 2 * LANES), lambda b, p: (layer, p, 0, 0, 0)),
                  pl.BlockSpec((None, 1, 2 * LANES), lambda b, p: (layer, 0, p))] + alias_spec,
        out_specs=[tok_spec, fin_spec, fin_spec],
        out_shape=[jax.ShapeDtypeStruct((S5_SEQS, S5_ROWS, D_SSM), F32), fin_shape, fin_shape],
        scratch_shapes=[pltpu.VMEM((seq * SUBLANES, LANES), F32), pltpu.VMEM((seq * SUBLANES, LANES), F32),
                        pltpu.VMEM((seq * SUBLANES, LANES), F32)],
        compiler_params=_params(2),
        name="s5_scan",
        **alias,
    )(us, h0r, h0i, ab_r, ab_i, wb, wcr, wci, d_skip, *alias_arg)


def _s5_pair_rows(st):
    x = st.reshape(S5_SEQS, 2, S5_PAIRS, 2, S5_STATES)
    return x.transpose(2, 1, 3, 0, 4).reshape(1, S5_PAIRS, 2, SUBLANES, S5_STATES)


def _glu_kernel(s_ref, w_ref, b_ref, o_ref, wbf):
    @pl.when(pl.program_id(0) == 0)
    def _():
        wbf[...] = w_ref[...].astype(BF)

    ssm = s_ref[...]
    gate = jnp.dot(ssm.astype(BF), wbf[...], preferred_element_type=F32) + b_ref[...]
    o_ref[...] = (ssm * jax.nn.sigmoid(gate)).astype(BF)


def _s5_glu(ssm, glu_w, glu_b, layer):
    def seg_index(i):
        slot, rblk = _segment_block(i)
        return slot, rblk, 0

    vec = pl.BlockSpec((None, 1, D_SSM), lambda i: (layer, 0, 0))
    return pl.pallas_call(
        _glu_kernel,
        grid=(T_ALL // SEQ,),
        in_specs=[pl.BlockSpec((None, SEQ, D_SSM), seg_index),
                  pl.BlockSpec((None, D_SSM, D_SSM), lambda i: (layer, 0, 0)), vec],
        out_specs=pl.BlockSpec((SEQ, D_SSM), lambda i: (i, 0)),
        out_shape=jax.ShapeDtypeStruct((T_ALL, D_SSM), BF),
        scratch_shapes=[pltpu.VMEM((D_SSM, D_SSM), BF)],
        compiler_params=_params(1),
        name="s5_glu",
    )(ssm, glu_w, glu_b.reshape(DEPTH, 1, D_SSM))


def _rope_tables():
    rows = DEC_SEQ // GRID_W
    t_row = jnp.repeat(jnp.arange(rows, dtype=F32), GRID_W)
    t_col = (jnp.arange(DEC_SEQ) % GRID_W).astype(F32)
    inv = 1.0 / (ROPE_THETA ** (jnp.arange(0, AXIS_DIM, 2, dtype=F32) / AXIS_DIM))
    ang_r = t_row[:, None] * inv
    ang_c = t_col[:, None] * inv
    cos = jnp.concatenate([jnp.cos(ang_r), jnp.cos(ang_r), jnp.cos(ang_c), jnp.cos(ang_c)], axis=-1)
    sin = jnp.concatenate([-jnp.sin(ang_r), jnp.sin(ang_r), -jnp.sin(ang_c), jnp.sin(ang_c)], axis=-1)
    return cos, sin


def _block_diag(blocks):
    n, r, c = blocks.shape[-3:]
    eye = jnp.eye(n, dtype=blocks.dtype)
    out = blocks[..., :, :, None, :] * eye[:, None, :, None]
    return out.reshape(blocks.shape[:-3] + (n * r, n * c))


def kernel(x_prompt, x_sample, c, cache_k, cache_v, state_lru, state_s5_re, state_s5_im, c_ctx, w_mod, b_mod, g_mix, w_in, q_gain, k_gain, rnn_conv_w, rnn_conv_b, lru_gate_w, lru_gate_b, lru_lambda, s5_a_re, s5_a_im, s5_log_dt, s5_b_re, s5_b_im, s5_c_re, s5_c_im, s5_d, glu_w, glu_b, w_pa, w_pr, w_ps, w_o, g_ffn, w_up, ffn_conv_w, ffn_conv_b, w_down):
    x = jnp.concatenate([x_prompt.reshape(T_CTX, D_MODEL), x_sample.reshape(T_LAT, D_MODEL)], axis=0)

    cvec = jnp.zeros((MOD_ROWS, D_MODEL), F32).at[0].set(c_ctx).at[1:1 + DEC_BATCH].set(c)
    mod = _modulation(cvec, w_mod, b_mod)
    cos_t, sin_t = _rope_tables()

    gw = lru_gate_w.transpose(0, 3, 4, 1, 2, 5).reshape(DEPTH, N_RNN_BLOCKS, RNN_BLOCK, 4 * RNN_BLOCK).astype(BF)
    gb = (lru_gate_b.reshape(DEPTH, 2, 2, N_RNN_BLOCKS, RNN_BLOCK).transpose(0, 3, 1, 2, 4)
          .reshape(DEPTH, N_RNN_BLOCKS, 1, 4 * RNN_BLOCK))
    lam = lru_lambda.reshape(DEPTH, 2, N_RNN_BLOCKS, RNN_BLOCK).transpose(0, 2, 1, 3)
    cb = rnn_conv_b.reshape(DEPTH, 1, D_RNN)

    ab_r, ab_i, bbt_r, bbt_i = _s5_discretize(s5_a_re, s5_a_im, s5_log_dt, s5_b_re, s5_b_im)
    def pair_rows(ab):
        x = ab.reshape(DEPTH, 2, S5_PAIRS, 2, 1, S5_STATES)
        x = jnp.broadcast_to(x, (DEPTH, 2, S5_PAIRS, 2, S5_SEQS, S5_STATES))
        return x.transpose(0, 2, 1, 3, 4, 5).reshape(DEPTH, S5_PAIRS, 2, SUBLANES, S5_STATES)

    ab_r = pair_rows(ab_r)
    ab_i = pair_rows(ab_i)
    gshape = (DEPTH, 2, S5_TILES, S5_GROUPS, SSM_GROUP, SSM_STATE)
    wb = jnp.concatenate([_block_diag(bbt_r.reshape(gshape)), _block_diag(bbt_i.reshape(gshape))], axis=-1)
    wb = wb.reshape(DEPTH, 2, S5_PAIRS, 2 * LANES, 2 * S5_STATES)
    wb = wb.transpose(0, 2, 1, 3, 4).astype(BF)

    def pair_cols(c):
        x = _block_diag(jnp.swapaxes(c.reshape(gshape), -1, -2))
        x = x.reshape(DEPTH, 2, S5_PAIRS, 2, S5_STATES, LANES)
        return x.transpose(0, 2, 1, 4, 3, 5).reshape(DEPTH, S5_PAIRS, 2, S5_STATES, 2 * LANES).astype(BF)

    wcr = pair_cols(s5_c_re)
    wci = pair_cols(s5_c_im)
    d_skip = s5_d.reshape(DEPTH, 1, D_SSM)

    zeros_s5 = jnp.zeros((1, S5_PAIRS, 2, SUBLANES, S5_STATES), F32)
    cache_k = cache_k.reshape(DEC_BATCH, DEPTH, PAST_LEN, D_KV)
    cache_v = cache_v.reshape(DEC_BATCH, DEPTH, PAST_LEN, D_KV)

    w_pa_bf, w_pr_bf, w_ps_bf = w_pa.astype(BF), w_pr.astype(BF), w_ps.astype(BF)
    w_o_bf = w_o.astype(BF)
    w_down_bf = w_down.astype(BF)
    lat_s5_re = jnp.concatenate([_s5_pair_rows(state_s5_re[:, l]) for l in range(DEPTH)], axis=0)
    lat_s5_im = jnp.concatenate([_s5_pair_rows(state_s5_im[:, l]) for l in range(DEPTH)], axis=0)

    attn = jnp.zeros((T_ALL, D_ATTN), BF)
    rec = jnp.zeros((T_ALL, D_RNN), BF)
    s5 = jnp.zeros((S5_SEQS, S5_ROWS, D_SSM), F32)
    new_k = jnp.zeros((BATCH, DEPTH, SEQ, D_KV), F32)
    new_v = jnp.zeros((BATCH, DEPTH, SEQ, D_KV), F32)

    new_lru, new_sr, new_si = [], [], []
    h = _norm_mod(x, g_mix, mod, 0, 0, 1)
    for l in range(DEPTH):
        z, gates, us = _in_proj(h, w_in, l)

        attn, new_k, new_v = _attention(z, q_gain, k_gain, cache_k, cache_v, cos_t, sin_t, l,
                                        attn, new_k, new_v)

        rec, lru_f = _lru_ctx_call(z, rnn_conv_w, cb, gw, gb, lam, l, rec)
        rec, _ = _lru_call(z, state_lru[:, l], rnn_conv_w, cb, gw, gb, lam, l,
                           n_seq=DEC_BATCH, seq=DEC_SEQ, tok0=T_CTX, ct=256, filled=rec)
        new_lru.append(lru_f)

        s5, f_r, f_i = _s5_call(us, zeros_s5, zeros_s5, 0, ab_r, ab_i, wb, wcr, wci, d_skip, l,
                                n_blk=BATCH // S5_SEQS, seq=SEQ, row0=0, filled=s5)
        s5, _, _ = _s5_call(us, lat_s5_re, lat_s5_im, l, ab_r, ab_i, wb, wcr, wci, d_skip, l,
                            n_blk=DEC_BATCH // S5_SEQS, seq=DEC_SEQ, row0=S5_LAT_ROW0, filled=s5)
        ssm = _s5_glu(s5, glu_w, glu_b, l)
        new_sr.append(f_r)
        new_si.append(f_i)

        x, h2 = _merge_out(attn, rec, ssm, gates, w_pa_bf, w_pr_bf, w_ps_bf, w_o_bf, x, mod, g_ffn, l)

        hmid = _ffn_up(h2, w_up, ffn_conv_w, ffn_conv_b, l)
        if l + 1 < DEPTH:
            x, h = _residual_proj(hmid, w_down_bf, x, mod, l, 5, "ffn_down", tm=256,
                                  norm=(g_mix, l + 1, 0, 1))
        else:
            x, = _residual_proj(hmid, w_down_bf, x, mod, l, 5, "ffn_down", tm=256)

    def final_states(parts):
        st = jnp.stack(parts, axis=0).reshape(DEPTH, BATCH // S5_SEQS, S5_PAIRS, 2, 2, S5_SEQS, S5_STATES)
        st = st.transpose(1, 5, 0, 3, 2, 4, 6)
        return st.reshape(BATCH, DEPTH, 2, N_SSM_GROUPS, SSM_STATE)

    cache_shape = (BATCH, DEPTH, SEQ, N_KV_HEADS, HEAD_DIM)
    return (x[:T_CTX].reshape(BATCH, SEQ, D_MODEL), x[T_CTX:].reshape(DEC_BATCH, DEC_SEQ, D_MODEL),
            new_k.reshape(cache_shape), new_v.reshape(cache_shape), jnp.stack(new_lru, axis=1),
            final_states(new_sr), final_states(new_si))
```

```python
import functools
import math

import jax
import jax.numpy as jnp
from jax import lax
from jax.experimental import pallas as pl
from jax.experimental.pallas import tpu as pltpu

F32 = jnp.float32
BF = jnp.bfloat16

D_MODEL = 2048
BATCH = 32
SEQ = 256
DEPTH = 4
DEC_BATCH = 4
DEC_SEQ = 1024
PAST_LEN = 256
GRID_W = 64
N_HEADS = 8
N_KV_HEADS = 2
HEAD_DIM = 128
AXIS_DIM = HEAD_DIM // 2
ROPE_THETA = 10000.0
D_ATTN = N_HEADS * HEAD_DIM
D_KV = N_KV_HEADS * HEAD_DIM
D_RNN = D_MODEL // 2
RNN_BLOCK = 128
N_RNN_BLOCKS = D_RNN // RNN_BLOCK
RNN_CONV_W = 4
LRU_C = 8.0
D_SSM = D_MODEL // 2
SSM_GROUP = 16
N_SSM_GROUPS = D_SSM // SSM_GROUP
SSM_STATE = 64
D_FF = 2 * D_MODEL
N_MOD = 6
EPS = 1e-6
D_IN = D_ATTN + 2 * D_KV + 2 * D_RNN + D_SSM + 3 * D_MODEL

OFF_Q = 0
OFF_K = D_ATTN
OFF_V = OFF_K + D_KV
OFF_XR = OFF_V + D_KV
OFF_XG = OFF_XR + D_RNN
OFF_US = OFF_XG + D_RNN
OFF_GL = OFF_US + D_SSM

T_CTX = BATCH * SEQ
T_LAT = DEC_BATCH * DEC_SEQ
T_ALL = T_CTX + T_LAT
MOD_ROWS = 8

SUBLANES = 8
LANES = 128
VMEM_LIMIT = 56 * 1024 * 1024

S5_SEQS = SUBLANES // 2
S5_GROUPS = LANES // SSM_GROUP
S5_TILES = N_SSM_GROUPS // S5_GROUPS
S5_PAIRS = S5_TILES // 2
S5_STATES = S5_GROUPS * SSM_STATE
S5_CHUNK = 128
S5_GROUP = 32
S5_LAT_ROW0 = (BATCH // S5_SEQS) * SEQ
S5_ROWS = S5_LAT_ROW0 + DEC_SEQ


def _params(n_axes):
    return pltpu.CompilerParams(dimension_semantics=("arbitrary",) * n_axes,
                                vmem_limit_bytes=VMEM_LIMIT)


def _mod_row(first_token):
    return jnp.where(first_token < T_CTX, 0, 1 + jnp.maximum(first_token - T_CTX, 0) // DEC_SEQ)


def _gelu(x):
    return jax.nn.gelu(x)


def _sigmoid_tanh(x):
    return 0.5 * jnp.tanh(0.5 * x) + 0.5


def _mod_kernel(c_ref, w_ref, b_ref, o_ref):
    c = c_ref[...]
    s = (c * jax.nn.sigmoid(c)).astype(BF)
    o_ref[...] = jnp.dot(s, w_ref[...].astype(BF), preferred_element_type=F32) + b_ref[...]


def _modulation(cvec, w_mod, b_mod):
    tn = 1536
    n_out = N_MOD * D_MODEL
    return pl.pallas_call(
        _mod_kernel,
        grid=(DEPTH, n_out // tn),
        in_specs=[pl.BlockSpec((MOD_ROWS, D_MODEL), lambda l, j: (0, 0)),
                  pl.BlockSpec((None, D_MODEL, tn), lambda l, j: (l, 0, j)),
                  pl.BlockSpec((None, 1, tn), lambda l, j: (l, 0, j))],
        out_specs=pl.BlockSpec((None, MOD_ROWS, tn), lambda l, j: (l, 0, j)),
        out_shape=jax.ShapeDtypeStruct((DEPTH, MOD_ROWS, n_out), F32),
        compiler_params=_params(2),
        name="modulation",
    )(cvec, w_mod, b_mod.reshape(DEPTH, 1, n_out))


def _norm_kernel(x_ref, g_ref, sc_ref, sh_ref, o_ref, *, tm):
    r = _mod_row(pl.program_id(0) * tm)
    x = x_ref[...]
    y = x * lax.rsqrt(jnp.mean(x * x, axis=-1, keepdims=True) + EPS)
    y = y * g_ref[...]
    sc = sc_ref[pl.ds(r, 1), :]
    sh = sh_ref[pl.ds(r, 1), :]
    o_ref[...] = (y * (1.0 + sc) + sh).astype(BF)


def _norm_mod(x, gain, mod, layer, shift_idx, scale_idx):
    tm = 512
    return pl.pallas_call(
        functools.partial(_norm_kernel, tm=tm),
        grid=(T_ALL // tm,),
        in_specs=[pl.BlockSpec((tm, D_MODEL), lambda i: (i, 0)),
                  pl.BlockSpec((None, 1, D_MODEL), lambda i: (layer, 0, 0)),
                  pl.BlockSpec((None, MOD_ROWS, D_MODEL), lambda i: (layer, 0, scale_idx)),
                  pl.BlockSpec((None, MOD_ROWS, D_MODEL), lambda i: (layer, 0, shift_idx))],
        out_specs=pl.BlockSpec((tm, D_MODEL), lambda i: (i, 0)),
        out_shape=jax.ShapeDtypeStruct((T_ALL, D_MODEL), BF),
        compiler_params=_params(1),
        name="norm_mod",
    )(x, gain.reshape(DEPTH, 1, D_MODEL), mod, mod)


def _ws_kernel(*refs, n_a, n_w, n_extra, n_out, lhs_of, epilogue, tm):
    a_refs = refs[:n_a]
    w_refs = refs[n_a:n_a + n_w]
    extra = refs[n_a + n_w:n_a + n_w + n_extra]
    outs = refs[n_a + n_w + n_extra:n_a + n_w + n_extra + n_out]
    wbf = refs[n_a + n_w + n_extra + n_out:]
    j = pl.program_id(0)
    i = pl.program_id(1)

    @pl.when(i == 0)
    def _():
        for k in range(n_w):
            wbf[k][...] = w_refs[k][...].astype(BF)

    accs = [jnp.dot(a_refs[lhs_of[k]][...], wbf[k][...], preferred_element_type=F32)
            for k in range(n_w)]
    epilogue(accs, extra, outs, j, i * tm)


def _ws_matmul(a_list, w_list, extra, out_shapes, out_specs, epilogue, *, lhs_of, tm, tn, n_tiles, name,
               single_buffer_w=False):
    in_specs, args, scratch = [], [], []
    w_mode = dict(pipeline_mode=pl.Buffered(1)) if single_buffer_w else {}
    for a in a_list:
        in_specs.append(pl.BlockSpec((tm, a.shape[1]), lambda j, i: (i, 0)))
        args.append(a)
    for w, layer, off in w_list:
        kdim = w.shape[1]
        in_specs.append(pl.BlockSpec((None, kdim, tn), lambda j, i, layer=layer, off=off: (layer, 0, off + j),
                                     **w_mode))
        args.append(w)
        scratch.append(pltpu.VMEM((kdim, tn), BF))
    for arr, spec in extra:
        in_specs.append(spec)
        args.append(arr)
    kern = functools.partial(_ws_kernel, n_a=len(a_list), n_w=len(w_list), n_extra=len(extra),
                             n_out=len(out_shapes), lhs_of=lhs_of, epilogue=epilogue, tm=tm)
    return pl.pallas_call(
        kern,
        grid=(n_tiles, T_ALL // tm),
        in_specs=in_specs,
        out_specs=out_specs,
        out_shape=out_shapes,
        scratch_shapes=scratch,
        compiler_params=_params(2),
        name=name,
    )(*args)


def _segment_block(i):
    il = i - BATCH
    per_lat = DEC_SEQ // SEQ
    slot = jnp.where(i < BATCH, i % S5_SEQS, il // per_lat)
    rblk = jnp.where(i < BATCH, i // S5_SEQS, S5_LAT_ROW0 // SEQ + il % per_lat)
    return slot, rblk


def _store_acc(accs, extra, outs, j, row0):
    outs[0][...] = accs[0]


def _in_proj(h, w_in, layer):
    tm = 1024
    tn_a = 896
    z_a = _ws_matmul([h], [(w_in, layer, 0)], [],
                     [jax.ShapeDtypeStruct((T_ALL, OFF_US), F32)],
                     [pl.BlockSpec((tm, tn_a), lambda j, i: (i, j))],
                     _store_acc, lhs_of=(0,), tm=tm, tn=tn_a, n_tiles=OFF_US // tn_a, name="in_proj_a")[0]
    tn_g = 1536

    def gate_epilogue(accs, extra, outs, j, row0):
        outs[0][...] = jax.nn.sigmoid(accs[0]).astype(BF)

    z_g = _ws_matmul([h], [(w_in, layer, OFF_GL // tn_g)], [],
                     [jax.ShapeDtypeStruct((T_ALL, 3 * D_MODEL), BF)],
                     [pl.BlockSpec((tm, tn_g), lambda j, i: (i, j))],
                     gate_epilogue, lhs_of=(0,), tm=tm, tn=tn_g, n_tiles=3 * D_MODEL // tn_g,
                     name="in_proj_g")[0]

    tn_u = 512

    def us_index(j, i):
        slot, rblk = _segment_block(i)
        return slot, rblk, 0

    def us_epilogue(accs, extra, outs, j, row0):
        outs[0][...] = jnp.concatenate(accs, axis=1)

    us = _ws_matmul([h], [(w_in, layer, OFF_US // tn_u), (w_in, layer, OFF_US // tn_u + 1)], [],
                    [jax.ShapeDtypeStruct((S5_SEQS, S5_ROWS, D_SSM), F32)],
                    [pl.BlockSpec((None, SEQ, D_SSM), us_index)],
                    us_epilogue, lhs_of=(0, 0), tm=SEQ, tn=tn_u, n_tiles=1, name="in_proj_u")[0]
    return z_a, z_g, us


def _merge_out_kernel(attn_ref, rec_ref, s5_ref, g_ref, wglu_ref, bglu_ref, wpa_ref, wpr_ref, wps_ref, wo_ref,
                      x_ref, gate_ref, gain_ref, sc_ref, sh_ref, x_out, h_out, *, tm):
    r = _mod_row(pl.program_id(0) * tm)
    s5 = s5_ref[...]
    glu = jnp.dot(s5.astype(BF), wglu_ref[...], preferred_element_type=F32) + bglu_ref[...]
    ssm = (s5 * jax.nn.sigmoid(glu)).astype(BF)
    m = g_ref[:, 0:D_MODEL].astype(F32) * jnp.dot(attn_ref[...], wpa_ref[...], preferred_element_type=F32)
    m = m + (g_ref[:, D_MODEL:2 * D_MODEL].astype(F32)
             * jnp.dot(rec_ref[...], wpr_ref[...], preferred_element_type=F32))
    m = m + (g_ref[:, 2 * D_MODEL:3 * D_MODEL].astype(F32)
             * jnp.dot(ssm, wps_ref[...], preferred_element_type=F32))
    acc = jnp.dot(m.astype(BF), wo_ref[...], preferred_element_type=F32)
    x = x_ref[...] + gate_ref[pl.ds(r, 1), :] * acc
    x_out[...] = x
    y = x * lax.rsqrt(jnp.mean(x * x, axis=-1, keepdims=True) + EPS)
    y = y * gain_ref[...]
    h_out[...] = (y * (1.0 + sc_ref[pl.ds(r, 1), :]) + sh_ref[pl.ds(r, 1), :]).astype(BF)


def _merge_out(attn, rec, s5, gates, glu_w, glu_b, w_pa, w_pr, w_ps, w_o, x, mod, g_ffn, layer):
    tm = SEQ
    row = pl.BlockSpec((tm, D_MODEL), lambda i: (i, 0))
    branch = pl.BlockSpec((tm, D_ATTN), lambda i: (i, 0))
    resident = dict(pipeline_mode=pl.Buffered(1))
    wbr = pl.BlockSpec((None, D_ATTN, D_MODEL), lambda i: (layer, 0, 0), **resident)

    def mod_spec(idx):
        return pl.BlockSpec((None, MOD_ROWS, D_MODEL), lambda i: (layer, 0, idx))

    def seg_index(i):
        slot, rblk = _segment_block(i)
        return slot, rblk, 0

    return pl.pallas_call(
        functools.partial(_merge_out_kernel, tm=tm),
        grid=(T_ALL // tm,),
        in_specs=[branch, branch,
                  pl.BlockSpec((None, SEQ, D_SSM), seg_index),
                  pl.BlockSpec((tm, 3 * D_MODEL), lambda i: (i, 0)),
                  pl.BlockSpec((None, D_SSM, D_SSM), lambda i: (layer, 0, 0), **resident),
                  pl.BlockSpec((None, 1, D_SSM), lambda i: (layer, 0, 0)),
                  wbr, wbr, wbr,
                  pl.BlockSpec((None, D_MODEL, D_MODEL), lambda i: (layer, 0, 0), **resident),
                  row, mod_spec(2),
                  pl.BlockSpec((None, 1, D_MODEL), lambda i: (layer, 0, 0)), mod_spec(4), mod_spec(3)],
        out_specs=[row, row],
        out_shape=[jax.ShapeDtypeStruct((T_ALL, D_MODEL), F32), jax.ShapeDtypeStruct((T_ALL, D_MODEL), BF)],
        compiler_params=_params(1),
        name="merge_out",
    )(attn, rec, s5, gates, glu_w, glu_b.reshape(DEPTH, 1, D_SSM), w_pa, w_pr, w_ps, w_o, x, mod,
      g_ffn.reshape(DEPTH, 1, D_MODEL), mod, mod)


def _residual_kernel(a_ref, w_ref, x_ref, gate_ref, *rest, tm, with_norm):
    r = _mod_row(pl.program_id(0) * tm)
    acc = jnp.dot(a_ref[...], w_ref[...], preferred_element_type=F32)
    x = x_ref[...] + gate_ref[pl.ds(r, 1), :] * acc
    if not with_norm:
        rest[0][...] = x
        return
    g_ref, sc_ref, sh_ref, x_out, h_out = rest
    x_out[...] = x
    y = x * lax.rsqrt(jnp.mean(x * x, axis=-1, keepdims=True) + EPS)
    y = y * g_ref[...]
    h_out[...] = (y * (1.0 + sc_ref[pl.ds(r, 1), :]) + sh_ref[pl.ds(r, 1), :]).astype(BF)


def _residual_proj(a, w_bf, x, mod, layer, gate_idx, name, *, tm, norm=None):
    kdim = a.shape[1]
    row = pl.BlockSpec((tm, D_MODEL), lambda i: (i, 0))
    in_specs = [pl.BlockSpec((tm, kdim), lambda i: (i, 0)),
                pl.BlockSpec((None, kdim, D_MODEL), lambda i: (layer, 0, 0), pipeline_mode=pl.Buffered(1)),
                row,
                pl.BlockSpec((None, MOD_ROWS, D_MODEL), lambda i: (layer, 0, gate_idx))]
    args = [a, w_bf, x, mod]
    out_shape = [jax.ShapeDtypeStruct((T_ALL, D_MODEL), F32)]
    out_specs = [row]
    if norm is not None:
        gain, nl, shift_idx, scale_idx = norm
        in_specs += [pl.BlockSpec((None, 1, D_MODEL), lambda i: (nl, 0, 0)),
                     pl.BlockSpec((None, MOD_ROWS, D_MODEL), lambda i: (nl, 0, scale_idx)),
                     pl.BlockSpec((None, MOD_ROWS, D_MODEL), lambda i: (nl, 0, shift_idx))]
        args += [gain.reshape(DEPTH, 1, D_MODEL), mod, mod]
        out_shape.append(jax.ShapeDtypeStruct((T_ALL, D_MODEL), BF))
        out_specs.append(row)
    return pl.pallas_call(
        functools.partial(_residual_kernel, tm=tm, with_norm=norm is not None),
        grid=(T_ALL // tm,),
        in_specs=in_specs,
        out_specs=out_specs,
        out_shape=out_shape,
        compiler_params=_params(1),
        name=name,
    )(*args)


def _ffn_up(h2, w_up, conv_w, conv_b, layer):
    tm, tn = DEC_SEQ, 512

    def epilogue(accs, extra, outs, j, row0):
        a, b = accs
        seq_len = jnp.where(row0 < T_CTX, SEQ, DEC_SEQ)
        pos = lax.broadcasted_iota(jnp.int32, a.shape, 0) & (seq_len - 1)
        prev = jnp.where(pos == 0, 0.0, pltpu.roll(a, 1, 0))
        nxt = jnp.where(pos == seq_len - 1, 0.0, pltpu.roll(a, tm - 1, 0))
        cw = extra[0][...]
        a = cw[0:1] * prev + cw[1:2] * a + cw[2:3] * nxt + extra[1][...]
        outs[0][...] = (_gelu(a) * b).astype(BF)

    extra = [(conv_w, pl.BlockSpec((None, 3, tn), lambda j, i: (layer, 0, j))),
             (conv_b.reshape(DEPTH, 1, D_FF), pl.BlockSpec((None, 1, tn), lambda j, i: (layer, 0, j)))]
    return _ws_matmul([h2], [(w_up, layer, 0), (w_up, layer, D_FF // tn)], extra,
                      [jax.ShapeDtypeStruct((T_ALL, D_FF), BF)],
                      [pl.BlockSpec((tm, tn), lambda j, i: (i, j))],
                      epilogue, lhs_of=(0, 0), tm=tm, tn=tn, n_tiles=D_FF // tn, name="ffn_up")[0]


def _head_norm(x, gain):
    return x * lax.rsqrt(jnp.mean(x * x, axis=-1, keepdims=True) + EPS) * gain


def _rope(x, cos, sin_signed):
    lane = lax.broadcasted_iota(jnp.int32, x.shape, 1)
    first_half = (lane & (AXIS_DIM - 1)) < AXIS_DIM // 2
    partner = jnp.where(first_half, pltpu.roll(x, HEAD_DIM - AXIS_DIM // 2, 1),
                        pltpu.roll(x, AXIS_DIM // 2, 1))
    return x * cos + partner * sin_signed


def _scores(q, k):
    return lax.dot_general(q, k, (((1,), (1,)), ((), ())), preferred_element_type=F32) * (HEAD_DIM ** -0.5)


def _attn_ctx_kernel(q_ref, k_ref, v_ref, qg_ref, kg_ref, keep_a, keep_k, keep_v, o_ref, ko_ref, vo_ref):
    del keep_a, keep_k, keep_v
    grp = N_HEADS // N_KV_HEADS
    vo_ref[...] = v_ref[...]
    kn, vv = [], []
    for kv in range(N_KV_HEADS):
        ks = slice(kv * HEAD_DIM, (kv + 1) * HEAD_DIM)
        k = _head_norm(k_ref[:, ks], kg_ref[...])
        ko_ref[:, ks] = k
        kn.append(k.astype(BF))
        vv.append(v_ref[:, ks].astype(BF))

    def head_scores(h):
        qn = _head_norm(q_ref[:, h * HEAD_DIM:(h + 1) * HEAD_DIM], qg_ref[...]).astype(BF)
        return _scores(qn, kn[h // grp])

    s_next = head_scores(0)
    for h in range(N_HEADS):
        s = s_next
        if h + 1 < N_HEADS:
            s_next = head_scores(h + 1)
        p = jnp.exp(s - jnp.max(s, axis=-1, keepdims=True))
        den = jnp.sum(p, axis=-1, keepdims=True)
        o = jnp.dot(p.astype(BF), vv[h // grp], preferred_element_type=F32) / den
        o_ref[:, h * HEAD_DIM:(h + 1) * HEAD_DIM] = o.astype(BF)


def _attn_lat_kernel(q_ref, k_ref, v_ref, ck_ref, cv_ref, cosq_ref, sinq_ref, cosk_ref, sink_ref,
                     qg_ref, kg_ref, filled_ref, o_ref):
    del filled_ref
    grp = N_HEADS // N_KV_HEADS
    kn, vv, ck, cv = [], [], [], []
    for kv in range(N_KV_HEADS):
        ks = slice(kv * HEAD_DIM, (kv + 1) * HEAD_DIM)
        kn.append(_rope(_head_norm(k_ref[:, ks], kg_ref[...]), cosk_ref[...], sink_ref[...]).astype(BF))
        vv.append(v_ref[:, ks].astype(BF))
        ck.append(ck_ref[:, ks].astype(BF))
        cv.append(cv_ref[:, ks].astype(BF))

    def head_scores(h):
        q = _head_norm(q_ref[:, h * HEAD_DIM:(h + 1) * HEAD_DIM], qg_ref[...])
        qn = _rope(q, cosq_ref[...], sinq_ref[...]).astype(BF)
        return _scores(qn, ck[h // grp]), _scores(qn, kn[h // grp])

    s_next = head_scores(0)
    for h in range(N_HEADS):
        s1, s2 = s_next
        if h + 1 < N_HEADS:
            s_next = head_scores(h + 1)
        m = jnp.maximum(jnp.max(s1, axis=-1, keepdims=True), jnp.max(s2, axis=-1, keepdims=True))
        p1 = jnp.exp(s1 - m)
        p2 = jnp.exp(s2 - m)
        den = jnp.sum(p1, axis=-1, keepdims=True) + jnp.sum(p2, axis=-1, keepdims=True)
        o = (jnp.dot(p1.astype(BF), cv[h // grp], preferred_element_type=F32)
             + jnp.dot(p2.astype(BF), vv[h // grp], preferred_element_type=F32)) / den
        o_ref[:, h * HEAD_DIM:(h + 1) * HEAD_DIM] = o.astype(BF)


def _attention(z, q_gain, k_gain, cache_k, cache_v, cos_t, sin_t, layer, attn_buf, new_k, new_v):
    gains = [pl.BlockSpec((None, 1, HEAD_DIM), lambda *_: (layer, 0, 0))] * 2
    qg = q_gain.reshape(DEPTH, 1, HEAD_DIM)
    kg = k_gain.reshape(DEPTH, 1, HEAD_DIM)
    kcol, vcol = OFF_K // D_KV, OFF_V // D_KV
    keep = pl.BlockSpec(memory_space=pl.ANY)
    cache_spec = pl.BlockSpec((None, None, SEQ, D_KV), lambda b: (b, layer, 0, 0))
    attn, new_k, new_v = pl.pallas_call(
        _attn_ctx_kernel,
        grid=(BATCH,),
        in_specs=[pl.BlockSpec((SEQ, D_ATTN), lambda b: (b, 0)),
                  pl.BlockSpec((SEQ, D_KV), lambda b: (b, kcol)),
                  pl.BlockSpec((SEQ, D_KV), lambda b: (b, vcol))] + gains + [keep, keep, keep],
        out_specs=[pl.BlockSpec((SEQ, D_ATTN), lambda b: (b, 0)), cache_spec, cache_spec],
        out_shape=[jax.ShapeDtypeStruct((T_ALL, D_ATTN), BF),
                   jax.ShapeDtypeStruct((BATCH, DEPTH, SEQ, D_KV), F32),
                   jax.ShapeDtypeStruct((BATCH, DEPTH, SEQ, D_KV), F32)],
        input_output_aliases={5: 0, 6: 1, 7: 2},
        compiler_params=_params(1),
        name="attn_ctx",
    )(z, z, z, qg, kg, attn_buf, new_k, new_v)

    tq = 512
    nq = DEC_SEQ // tq
    lat0 = T_CTX // DEC_SEQ
    attn = pl.pallas_call(
        _attn_lat_kernel,
        grid=(DEC_BATCH, nq),
        in_specs=[pl.BlockSpec((tq, D_ATTN), lambda b, i: ((lat0 + b) * nq + i, 0)),
                  pl.BlockSpec((DEC_SEQ, D_KV), lambda b, i: (lat0 + b, kcol)),
                  pl.BlockSpec((DEC_SEQ, D_KV), lambda b, i: (lat0 + b, vcol)),
                  pl.BlockSpec((None, None, PAST_LEN, D_KV), lambda b, i: (b, layer, 0, 0)),
                  pl.BlockSpec((None, None, PAST_LEN, D_KV), lambda b, i: (b, layer, 0, 0)),
                  pl.BlockSpec((tq, HEAD_DIM), lambda b, i: (i, 0)),
                  pl.BlockSpec((tq, HEAD_DIM), lambda b, i: (i, 0)),
                  pl.BlockSpec((DEC_SEQ, HEAD_DIM), lambda b, i: (0, 0)),
                  pl.BlockSpec((DEC_SEQ, HEAD_DIM), lambda b, i: (0, 0))] + gains
                 + [pl.BlockSpec(memory_space=pl.ANY)],
        out_specs=pl.BlockSpec((tq, D_ATTN), lambda b, i: ((lat0 + b) * nq + i, 0)),
        out_shape=jax.ShapeDtypeStruct((T_ALL, D_ATTN), BF),
        input_output_aliases={11: 0},
        compiler_params=_params(2),
        name="attn_lat",
    )(z, z, z, cache_k, cache_v, cos_t, sin_t, cos_t, sin_t, qg, kg, attn)
    return attn, new_k, new_v


def _lru_kernel(xr_ref, xg_ref, h0_ref, cw_ref, cb_ref, gw_ref, gb_ref, lam_ref, *rest, seq, ct):
    o_ref, hf_ref, a_s, b_s, hfw_s, hbw_s = rest[-6:]
    n_blocks = ct // RNN_BLOCK
    n_tiles = seq // SUBLANES
    row = lax.broadcasted_iota(jnp.int32, (seq, ct), 0)
    xr = xr_ref[...]

    def shifted(x, k):
        if k > 0:
            return jnp.where(row >= k, pltpu.roll(x, k, 0), 0.0)
        return jnp.where(row < seq + k, pltpu.roll(x, seq + k, 0), 0.0)

    cw = cw_ref[...]
    xc = (cb_ref[...] + cw[0:1] * shifted(xr, 2) + cw[1:2] * shifted(xr, 1) + cw[2:3] * xr
          + cw[3:4] * shifted(xr, -1))

    for kb in range(n_blocks):
        cs = slice(kb * RNN_BLOCK, (kb + 1) * RNN_BLOCK)
        xcb = xc[:, cs]
        gates = jnp.dot(xcb.astype(BF), gw_ref[kb], preferred_element_type=F32) + gb_ref[kb]
        for d in range(2):
            a, b = _lru_gate_terms(xcb, gates, lam_ref, kb, d)
            a_s[d, :, cs] = a
            b_s[d, :, cs] = b

    sub = lax.broadcasted_iota(jnp.int32, (SUBLANES, ct), 0)

    def tile_scan(a, b, forward):
        for k in (1, 2, 4):
            if forward:
                keep = sub >= k
                shift = k
            else:
                keep = sub < SUBLANES - k
                shift = SUBLANES - k
            a_n = jnp.where(keep, pltpu.roll(a, shift, 0), 1.0)
            b_n = jnp.where(keep, pltpu.roll(b, shift, 0), 0.0)
            b = a * b_n + b
            a = a * a_n
        return a, b

    def body(j, carry):
        hf, hb = carry
        r0 = pl.multiple_of(j * SUBLANES, SUBLANES)
        a, b = tile_scan(a_s[0, pl.ds(r0, SUBLANES), :], b_s[0, pl.ds(r0, SUBLANES), :], True)
        h = a * hf + b
        hfw_s[pl.ds(r0, SUBLANES), :] = h
        hf = h[SUBLANES - 1:SUBLANES, :]
        r1 = pl.multiple_of((n_tiles - 1 - j) * SUBLANES, SUBLANES)
        a, b = tile_scan(a_s[1, pl.ds(r1, SUBLANES), :], b_s[1, pl.ds(r1, SUBLANES), :], False)
        h = a * hb + b
        hbw_s[pl.ds(r1, SUBLANES), :] = h
        hb = h[0:1, :]
        return hf, hb

    hf, hb = lax.fori_loop(0, n_tiles, body, (h0_ref[0:1, :], h0_ref[1:2, :]))
    hf_ref[0:1, :] = hf
    hf_ref[1:2, :] = hb
    o_ref[...] = (_gelu(xg_ref[...]) * (hfw_s[...] + hbw_s[...])).astype(BF)


def _lru_call(z, h0, cw, cb, gw, gb, lam, layer, *, n_seq, seq, tok0, ct, filled=None):
    nc = D_RNN // ct
    s0 = tok0 // seq
    blk = ct // RNN_BLOCK
    alias = {} if filled is None else dict(input_output_aliases={8: 0})
    alias_spec = [] if filled is None else [pl.BlockSpec(memory_space=pl.ANY)]
    alias_arg = [] if filled is None else [filled]
    return pl.pallas_call(
        functools.partial(_lru_kernel, seq=seq, ct=ct),
        grid=(n_seq, nc),
        in_specs=[pl.BlockSpec((seq, ct), lambda s, c: (s0 + s, OFF_XR // ct + c)),
                  pl.BlockSpec((seq, ct), lambda s, c: (s0 + s, OFF_XG // ct + c)),
                  pl.BlockSpec((None, None, 2, ct), lambda s, c: (s, layer, 0, c)),
                  pl.BlockSpec((None, RNN_CONV_W, ct), lambda s, c: (layer, 0, c)),
                  pl.BlockSpec((None, 1, ct), lambda s, c: (layer, 0, c)),
                  pl.BlockSpec((None, blk, RNN_BLOCK, 4 * RNN_BLOCK), lambda s, c: (layer, c, 0, 0)),
                  pl.BlockSpec((None, blk, 1, 4 * RNN_BLOCK), lambda s, c: (layer, c, 0, 0)),
                  pl.BlockSpec((None, blk, 2, RNN_BLOCK), lambda s, c: (layer, c, 0, 0))] + alias_spec,
        out_specs=[pl.BlockSpec((seq, ct), lambda s, c: (s0 + s, c)),
                   pl.BlockSpec((None, 2, ct), lambda s, c: (s, 0, c))],
        out_shape=[jax.ShapeDtypeStruct((T_ALL, D_RNN), BF),
                   jax.ShapeDtypeStruct((n_seq, 2, D_RNN), F32)],
        scratch_shapes=[pltpu.VMEM((2, seq, ct), F32), pltpu.VMEM((2, seq, ct), F32),
                        pltpu.VMEM((seq, ct), F32), pltpu.VMEM((seq, ct), F32)],
        compiler_params=_params(2),
        name="rg_lru",
        **alias,
    )(z, z, h0, cw, cb, gw, gb, lam, *alias_arg)


def _lru_gate_terms(xc, gates, lam_ref, kb, d):
    base = d * 2 * RNN_BLOCK
    r = _sigmoid_tanh(gates[:, base:base + RNN_BLOCK])
    ig = _sigmoid_tanh(gates[:, base + RNN_BLOCK:base + 2 * RNN_BLOCK])
    lam = lam_ref[kb, d:d + 1, :]
    a = jnp.exp(r * (-LRU_C * jnp.log1p(jnp.exp(-lam))))
    om = 1.0 - a * a
    root = jnp.where(om > 0.0, om * lax.rsqrt(om), 0.0)
    return a, root * (ig * xc)


LRU_CTX_SEQS = SUBLANES
LRU_PAD_FRONT = (RNN_CONV_W // 2) * SUBLANES
LRU_PAD_BACK = (RNN_CONV_W - 1 - RNN_CONV_W // 2) * SUBLANES


def _lru_ctx_kernel(xr_ref, xg_ref, cw_ref, cb_ref, gw_ref, gb_ref, lam_ref, keep_ref, o_ref, hf_ref,
                    xs, a_s, b_s, hfw, hbw, *, ct):
    del keep_ref
    n_blocks = ct // RNN_BLOCK
    rows = SEQ * SUBLANES

    @pl.when((pl.program_id(0) == 0) & (pl.program_id(1) == 0))
    def _():
        xs[...] = jnp.zeros_like(xs)

    for kb in range(n_blocks):
        cs = slice(kb * RNN_BLOCK, (kb + 1) * RNN_BLOCK)
        for b in range(LRU_CTX_SEQS):
            xs[kb, pl.ds(LRU_PAD_FRONT + b, SEQ, stride=SUBLANES), :] = xr_ref[b, :, cs]

    for kb in range(n_blocks):
        cs = slice(kb * RNN_BLOCK, (kb + 1) * RNN_BLOCK)
        xc = cb_ref[:, cs]
        for j in range(RNN_CONV_W):
            xc = xc + cw_ref[j:j + 1, cs] * xs[kb, j * SUBLANES:j * SUBLANES + rows, :]
        gates = jnp.dot(xc.astype(BF), gw_ref[kb], preferred_element_type=F32) + gb_ref[kb]
        for d in range(2):
            a, b = _lru_gate_terms(xc, gates, lam_ref, kb, d)
            a_s[d, kb] = a
            b_s[d, kb] = b

    def body(t, carry):
        hf, hb = carry
        r0 = pl.multiple_of(t * SUBLANES, SUBLANES)
        r1 = pl.multiple_of((SEQ - 1 - t) * SUBLANES, SUBLANES)
        new_f, new_b = [], []
        for kb in range(n_blocks):
            f = a_s[0, kb, pl.ds(r0, SUBLANES), :] * hf[kb] + b_s[0, kb, pl.ds(r0, SUBLANES), :]
            hfw[kb, pl.ds(r0, SUBLANES), :] = f
            new_f.append(f)
            g = a_s[1, kb, pl.ds(r1, SUBLANES), :] * hb[kb] + b_s[1, kb, pl.ds(r1, SUBLANES), :]
            hbw[kb, pl.ds(r1, SUBLANES), :] = g
            new_b.append(g)
        return tuple(new_f), tuple(new_b)

    zero = tuple(jnp.zeros((SUBLANES, RNN_BLOCK), F32) for _ in range(n_blocks))
    hf, hb = lax.fori_loop(0, SEQ, body, (zero, zero), unroll=8)
    for kb in range(n_blocks):
        cs = slice(kb * RNN_BLOCK, (kb + 1) * RNN_BLOCK)
        hf_ref[0, :, cs] = hf[kb]
        hf_ref[1, :, cs] = hb[kb]
        for b in range(LRU_CTX_SEQS):
            h = (hfw[kb, pl.ds(b, SEQ, stride=SUBLANES), :] + hbw[kb, pl.ds(b, SEQ, stride=SUBLANES), :])
            o_ref[b, :, cs] = (_gelu(xg_ref[b, :, cs]) * h).astype(BF)


def _lru_ctx_call(z, cw, cb, gw, gb, lam, layer, filled):
    ct = 256
    blk = ct // RNN_BLOCK
    n_seg = T_ALL // SEQ
    z3 = z.reshape(n_seg, SEQ, z.shape[1])
    rows = SEQ * SUBLANES
    pad_rows = LRU_PAD_FRONT + rows + LRU_PAD_BACK
    tok = pl.BlockSpec((LRU_CTX_SEQS, SEQ, ct), lambda s, c: (s, 0, c))
    rec, fin = pl.pallas_call(
        functools.partial(_lru_ctx_kernel, ct=ct),
        grid=(BATCH // LRU_CTX_SEQS, D_RNN // ct),
        in_specs=[pl.BlockSpec((LRU_CTX_SEQS, SEQ, ct), lambda s, c: (s, 0, OFF_XR // ct + c)),
                  pl.BlockSpec((LRU_CTX_SEQS, SEQ, ct), lambda s, c: (s, 0, OFF_XG // ct + c)),
                  pl.BlockSpec((None, RNN_CONV_W, ct), lambda s, c: (layer, 0, c)),
                  pl.BlockSpec((None, 1, ct), lambda s, c: (layer, 0, c)),
                  pl.BlockSpec((None, blk, RNN_BLOCK, 4 * RNN_BLOCK), lambda s, c: (layer, c, 0, 0)),
                  pl.BlockSpec((None, blk, 1, 4 * RNN_BLOCK), lambda s, c: (layer, c, 0, 0)),
                  pl.BlockSpec((None, blk, 2, RNN_BLOCK), lambda s, c: (layer, c, 0, 0)),
                  pl.BlockSpec(memory_space=pl.ANY)],
        out_specs=[tok, pl.BlockSpec((None, 2, LRU_CTX_SEQS, ct), lambda s, c: (s, 0, 0, c))],
        out_shape=[jax.ShapeDtypeStruct((n_seg, SEQ, D_RNN), BF),
                   jax.ShapeDtypeStruct((BATCH // LRU_CTX_SEQS, 2, LRU_CTX_SEQS, D_RNN), F32)],
        scratch_shapes=[pltpu.VMEM((blk, pad_rows, RNN_BLOCK), F32),
                        pltpu.VMEM((2, blk, rows, RNN_BLOCK), F32), pltpu.VMEM((2, blk, rows, RNN_BLOCK), F32),
                        pltpu.VMEM((blk, rows, RNN_BLOCK), F32), pltpu.VMEM((blk, rows, RNN_BLOCK), F32)],
        input_output_aliases={7: 0},
        compiler_params=_params(2),
        name="rg_lru_ctx",
    )(z3, z3, cw, cb, gw, gb, lam, filled.reshape(n_seg, SEQ, D_RNN))
    return rec.reshape(T_ALL, D_RNN), fin


def _s5_disc_kernel(are_ref, aim_ref, ldt_ref, bre_ref, bim_ref, abr_ref, abi_ref, bbr_ref, bbi_ref):
    lam_r = jnp.minimum(are_ref[...], -1e-4)
    lam_i = aim_ref[...]
    dt = jnp.exp(ldt_ref[...])
    mag = jnp.exp(lam_r * dt)
    ab_r = mag * jnp.cos(lam_i * dt)
    ab_i = mag * jnp.sin(lam_i * dt)
    den = lam_r * lam_r + lam_i * lam_i
    q_r = ((ab_r - 1.0) * lam_r + ab_i * lam_i) / den
    q_i = (ab_i * lam_r - (ab_r - 1.0) * lam_i) / den
    abr_ref[...] = ab_r
    abi_ref[...] = ab_i
    br = bre_ref[...]
    bi = bim_ref[...]
    bbr_ref[...] = q_r[:, None, :] * br - q_i[:, None, :] * bi
    bbi_ref[...] = q_r[:, None, :] * bi + q_i[:, None, :] * br


def _s5_discretize(a_re, a_im, log_dt, b_re, b_im):
    n = DEPTH * 2 * N_SSM_GROUPS
    bt_re = jnp.swapaxes(b_re, -1, -2).reshape(n, SSM_GROUP, SSM_STATE)
    bt_im = jnp.swapaxes(b_im, -1, -2).reshape(n, SSM_GROUP, SSM_STATE)
    vec = jax.ShapeDtypeStruct((n, SSM_STATE), F32)
    mat = jax.ShapeDtypeStruct((n, SSM_GROUP, SSM_STATE), F32)
    return pl.pallas_call(
        _s5_disc_kernel,
        out_shape=[vec, vec, mat, mat],
        compiler_params=pltpu.CompilerParams(vmem_limit_bytes=VMEM_LIMIT),
        name="s5_discretize",
    )(a_re.reshape(n, SSM_STATE), a_im.reshape(n, SSM_STATE), log_dt.reshape(n, 1), bt_re, bt_im)


def _s5_kernel(u_ref, h0r_ref, h0i_ref, ar_ref, ai_ref, wb_ref, wcr_ref, wci_ref, d_ref, *rest, seq):
    y_ref, fr_ref, fi_ref, lhs_a, lhs_b, acc_s = rest[-6:]
    rows = S5_CHUNK * SUBLANES
    n_chunks = seq // S5_CHUNK

    @pl.when((pl.program_id(0) == 0) & (pl.program_id(1) == 0))
    def _():
        lhs_a[...] = jnp.zeros_like(lhs_a)
        lhs_b[...] = jnp.zeros_like(lhs_b)

    for b in range(S5_SEQS):
        lhs_a[pl.ds(b, seq, stride=SUBLANES), :] = u_ref[b, :, 0:LANES]
        lhs_b[pl.ds(S5_SEQS + b, seq, stride=SUBLANES), :] = u_ref[b, :, LANES:2 * LANES]

    grows = S5_GROUP * SUBLANES
    n_groups = S5_CHUNK // S5_GROUP

    def bu_rows(d, r0, n):
        lhs = jnp.concatenate([lhs_a[pl.ds(r0, n), :], lhs_b[pl.ds(r0, n), :]], axis=1).astype(BF)
        return jnp.dot(lhs, wb_ref[d], preferred_element_type=F32)

    def add_output(d, r0, h_re, h_im):
        y = (jnp.dot(h_re.astype(BF), wcr_ref[d], preferred_element_type=F32)
             - jnp.dot(h_im.astype(BF), wci_ref[d], preferred_element_type=F32))
        rid = lax.broadcasted_iota(jnp.int32, (h_re.shape[0], LANES), 0)
        y = jnp.where((rid & (SUBLANES - 1)) < S5_SEQS, y[:, 0:LANES], y[:, LANES:2 * LANES])
        if d == 0:
            acc_s[pl.ds(r0, h_re.shape[0]), :] = y
        else:
            acc_s[pl.ds(r0, h_re.shape[0]), :] += y

    for d in range(2):
        ar = ar_ref[d]
        ai = ai_ref[d]

        def chunk(c, carry, d=d, ar=ar, ai=ai):
            cc = c if d == 0 else n_chunks - 1 - c
            r0 = pl.multiple_of(cc * rows, rows)
            hr, hi = carry
            order = [g if d == 0 else n_groups - 1 - g for g in range(n_groups)]
            bu_next = bu_rows(d, r0 + order[0] * grows, grows)
            for g in range(n_groups):
                q = r0 + order[g] * grows
                bu = bu_next
                if g + 1 < n_groups:
                    bu_next = bu_rows(d, r0 + order[g + 1] * grows, grows)
                hs_r = [None] * S5_GROUP
                hs_i = [None] * S5_GROUP
                for s in range(S5_GROUP):
                    ss = s if d == 0 else S5_GROUP - 1 - s
                    br = bu[ss * SUBLANES:(ss + 1) * SUBLANES, 0:S5_STATES]
                    bi = bu[ss * SUBLANES:(ss + 1) * SUBLANES, S5_STATES:2 * S5_STATES]
                    hr, hi = ar * hr - ai * hi + br, ar * hi + ai * hr + bi
                    hs_r[ss] = hr
                    hs_i[ss] = hi
                add_output(d, q, jnp.concatenate(hs_r, axis=0), jnp.concatenate(hs_i, axis=0))
            return hr, hi

        hr, hi = lax.fori_loop(0, n_chunks, chunk, (h0r_ref[d], h0i_ref[d]))
        fr_ref[d] = hr
        fi_ref[d] = hi

    for b in range(S5_SEQS):
        y = jnp.concatenate([acc_s[pl.ds(b, seq, stride=SUBLANES), :],
                             acc_s[pl.ds(S5_SEQS + b, seq, stride=SUBLANES), :]], axis=1)
        y_ref[b] = _gelu(y + d_ref[...] * u_ref[b])


def _s5_call(us, h0r, h0i, h0_idx, ab_r, ab_i, wb, wcr, wci, d_skip, layer, *, n_blk, seq, row0, filled=None):
    blk0 = row0 // seq
    st = (None, None, 2, SUBLANES, S5_STATES)
    tok_spec = pl.BlockSpec((S5_SEQS, seq, 2 * LANES), lambda b, p: (0, blk0 + b, p))
    h0_spec = pl.BlockSpec(st, lambda b, p: (h0_idx, p, 0, 0, 0))
    par_spec = pl.BlockSpec(st, lambda b, p: (layer, p, 0, 0, 0))
    fin_spec = pl.BlockSpec(st, lambda b, p: (b, p, 0, 0, 0))
    fin_shape = jax.ShapeDtypeStruct((n_blk, S5_PAIRS, 2, SUBLANES, S5_STATES), F32)
    alias = {} if filled is None else dict(input_output_aliases={9: 0})
    alias_spec = [] if filled is None else [pl.BlockSpec(memory_space=pl.ANY)]
    alias_arg = [] if filled is None else [filled]
    return pl.pallas_call(
        functools.partial(_s5_kernel, seq=seq),
        grid=(n_blk, S5_PAIRS),
        in_specs=[tok_spec, h0_spec, h0_spec, par_spec, par_spec,
                  pl.BlockSpec((None, None, 2, 2 * LANES, 2 * S5_STATES), lambda b, p: (layer, p, 0, 0, 0)),
                  pl.BlockSpec((None, None, 2, S5_STATES, 2 * LANES), lambda b, p: (layer, p, 0, 0, 0)),
                  pl.BlockSpec((None, None, 2, S5_STATES, 2 * LANES), lambda b, p: (layer, p, 0, 0, 0)),
                  pl.BlockSpec((None, 1, 2 * LANES), lambda b, p: (layer, 0, p))] + alias_spec,
        out_specs=[tok_spec, fin_spec, fin_spec],
        out_shape=[jax.ShapeDtypeStruct((S5_SEQS, S5_ROWS, D_SSM), F32), fin_shape, fin_shape],
        scratch_shapes=[pltpu.VMEM((seq * SUBLANES, LANES), F32), pltpu.VMEM((seq * SUBLANES, LANES), F32),
                        pltpu.VMEM((seq * SUBLANES, LANES), F32)],
        compiler_params=_params(2),
        name="s5_scan",
        **alias,
    )(us, h0r, h0i, ab_r, ab_i, wb, wcr, wci, d_skip, *alias_arg)


def _s5_pair_rows(st):
    x = st.reshape(S5_SEQS, DEPTH, 2, S5_PAIRS, 2, S5_STATES)
    return x.transpose(1, 3, 2, 4, 0, 5).reshape(DEPTH, S5_PAIRS, 2, SUBLANES, S5_STATES)


def _rope_tables():
    rows = DEC_SEQ // GRID_W
    t_row = jnp.repeat(jnp.arange(rows, dtype=F32), GRID_W)
    t_col = (jnp.arange(DEC_SEQ) % GRID_W).astype(F32)
    inv = 1.0 / (ROPE_THETA ** (jnp.arange(0, AXIS_DIM, 2, dtype=F32) / AXIS_DIM))
    ang_r = t_row[:, None] * inv
    ang_c = t_col[:, None] * inv
    cos = jnp.concatenate([jnp.cos(ang_r), jnp.cos(ang_r), jnp.cos(ang_c), jnp.cos(ang_c)], axis=-1)
    sin = jnp.concatenate([-jnp.sin(ang_r), jnp.sin(ang_r), -jnp.sin(ang_c), jnp.sin(ang_c)], axis=-1)
    return cos, sin


def _block_diag(blocks):
    n, r, c = blocks.shape[-3:]
    eye = jnp.eye(n, dtype=blocks.dtype)
    out = blocks[..., :, :, None, :] * eye[:, None, :, None]
    return out.reshape(blocks.shape[:-3] + (n * r, n * c))


def kernel(x_prompt, x_sample, c, cache_k, cache_v, state_lru, state_s5_re, state_s5_im, c_ctx, w_mod, b_mod, g_mix, w_in, q_gain, k_gain, rnn_conv_w, rnn_conv_b, lru_gate_w, lru_gate_b, lru_lambda, s5_a_re, s5_a_im, s5_log_dt, s5_b_re, s5_b_im, s5_c_re, s5_c_im, s5_d, glu_w, glu_b, w_pa, w_pr, w_ps, w_o, g_ffn, w_up, ffn_conv_w, ffn_conv_b, w_down):
    x = jnp.concatenate([x_prompt.reshape(T_CTX, D_MODEL), x_sample.reshape(T_LAT, D_MODEL)], axis=0)

    cvec = jnp.zeros((MOD_ROWS, D_MODEL), F32).at[0].set(c_ctx).at[1:1 + DEC_BATCH].set(c)
    mod = _modulation(cvec, w_mod, b_mod)
    cos_t, sin_t = _rope_tables()

    gw = lru_gate_w.transpose(0, 3, 4, 1, 2, 5).reshape(DEPTH, N_RNN_BLOCKS, RNN_BLOCK, 4 * RNN_BLOCK).astype(BF)
    gb = (lru_gate_b.reshape(DEPTH, 2, 2, N_RNN_BLOCKS, RNN_BLOCK).transpose(0, 3, 1, 2, 4)
          .reshape(DEPTH, N_RNN_BLOCKS, 1, 4 * RNN_BLOCK))
    lam = lru_lambda.reshape(DEPTH, 2, N_RNN_BLOCKS, RNN_BLOCK).transpose(0, 2, 1, 3)
    cb = rnn_conv_b.reshape(DEPTH, 1, D_RNN)

    ab_r, ab_i, bbt_r, bbt_i = _s5_discretize(s5_a_re, s5_a_im, s5_log_dt, s5_b_re, s5_b_im)
    def pair_rows(ab):
        x = ab.reshape(DEPTH, 2, S5_PAIRS, 2, 1, S5_STATES)
        x = jnp.broadcast_to(x, (DEPTH, 2, S5_PAIRS, 2, S5_SEQS, S5_STATES))
        return x.transpose(0, 2, 1, 3, 4, 5).reshape(DEPTH, S5_PAIRS, 2, SUBLANES, S5_STATES)

    ab_r = pair_rows(ab_r)
    ab_i = pair_rows(ab_i)
    gshape = (DEPTH, 2, S5_TILES, S5_GROUPS, SSM_GROUP, SSM_STATE)
    wb = jnp.concatenate([_block_diag(bbt_r.reshape(gshape)), _block_diag(bbt_i.reshape(gshape))], axis=-1)
    wb = wb.reshape(DEPTH, 2, S5_PAIRS, 2 * LANES, 2 * S5_STATES)
    wb = wb.transpose(0, 2, 1, 3, 4).astype(BF)

    def pair_cols(c):
        x = _block_diag(jnp.swapaxes(c.reshape(gshape), -1, -2))
        x = x.reshape(DEPTH, 2, S5_PAIRS, 2, S5_STATES, LANES)
        return x.transpose(0, 2, 1, 4, 3, 5).reshape(DEPTH, S5_PAIRS, 2, S5_STATES, 2 * LANES).astype(BF)

    wcr = pair_cols(s5_c_re)
    wci = pair_cols(s5_c_im)
    d_skip = s5_d.reshape(DEPTH, 1, D_SSM)

    zeros_s5 = jnp.zeros((1, S5_PAIRS, 2, SUBLANES, S5_STATES), F32)
    cache_k = cache_k.reshape(DEC_BATCH, DEPTH, PAST_LEN, D_KV)
    cache_v = cache_v.reshape(DEC_BATCH, DEPTH, PAST_LEN, D_KV)

    w_pa_bf, w_pr_bf, w_ps_bf = w_pa.astype(BF), w_pr.astype(BF), w_ps.astype(BF)
    w_o_bf = w_o.astype(BF)
    w_down_bf = w_down.astype(BF)
    glu_w_bf = glu_w.astype(BF)
    lat_s5_re = _s5_pair_rows(state_s5_re)
    lat_s5_im = _s5_pair_rows(state_s5_im)

    attn = jnp.zeros((T_ALL, D_ATTN), BF)
    rec = jnp.zeros((T_ALL, D_RNN), BF)
    s5 = jnp.zeros((S5_SEQS, S5_ROWS, D_SSM), F32)
    new_k = jnp.zeros((BATCH, DEPTH, SEQ, D_KV), F32)
    new_v = jnp.zeros((BATCH, DEPTH, SEQ, D_KV), F32)

    new_lru, new_sr, new_si = [], [], []
    h = _norm_mod(x, g_mix, mod, 0, 0, 1)
    for l in range(DEPTH):
        z, gates, us = _in_proj(h, w_in, l)

        attn, new_k, new_v = _attention(z, q_gain, k_gain, cache_k, cache_v, cos_t, sin_t, l,
                                        attn, new_k, new_v)

        rec, lru_f = _lru_ctx_call(z, rnn_conv_w, cb, gw, gb, lam, l, rec)
        rec, _ = _lru_call(z, state_lru, rnn_conv_w, cb, gw, gb, lam, l,
                           n_seq=DEC_BATCH, seq=DEC_SEQ, tok0=T_CTX, ct=256, filled=rec)
        new_lru.append(lru_f)

        s5, f_r, f_i = _s5_call(us, zeros_s5, zeros_s5, 0, ab_r, ab_i, wb, wcr, wci, d_skip, l,
                                n_blk=BATCH // S5_SEQS, seq=SEQ, row0=0, filled=s5)
        s5, _, _ = _s5_call(us, lat_s5_re, lat_s5_im, l, ab_r, ab_i, wb, wcr, wci, d_skip, l,
                            n_blk=DEC_BATCH // S5_SEQS, seq=DEC_SEQ, row0=S5_LAT_ROW0, filled=s5)
        new_sr.append(f_r)
        new_si.append(f_i)

        x, h2 = _merge_out(attn, rec, s5, gates, glu_w_bf, glu_b, w_pa_bf, w_pr_bf, w_ps_bf, w_o_bf,
                           x, mod, g_ffn, l)

        hmid = _ffn_up(h2, w_up, ffn_conv_w, ffn_conv_b, l)
        if l + 1 < DEPTH:
            x, h = _residual_proj(hmid, w_down_bf, x, mod, l, 5, "ffn_down", tm=256,
                                  norm=(g_mix, l + 1, 0, 1))
        else:
            x, = _residual_proj(hmid, w_down_bf, x, mod, l, 5, "ffn_down", tm=256)

    def final_states(parts):
        st = jnp.stack(parts, axis=0).reshape(DEPTH, BATCH // S5_SEQS, S5_PAIRS, 2, 2, S5_SEQS, S5_STATES)
        st = st.transpose(1, 5, 0, 3, 2, 4, 6)
        return st.reshape(BATCH, DEPTH, 2, N_SSM_GROUPS, SSM_STATE)

    cache_shape = (BATCH, DEPTH, SEQ, N_KV_HEADS, HEAD_DIM)
    return (x[:T_CTX].reshape(BATCH, SEQ, D_MODEL), x[T_CTX:].reshape(DEC_BATCH, DEC_SEQ, D_MODEL),
            new_k.reshape(cache_shape), new_v.reshape(cache_shape),
            jnp.stack(new_lru, axis=0).transpose(1, 3, 0, 2, 4).reshape(BATCH, DEPTH, 2, D_RNN),
            final_states(new_sr), final_states(new_si))
```

```python
import functools
import math

import jax
import jax.numpy as jnp
from jax import lax
from jax.experimental import pallas as pl
from jax.experimental.pallas import tpu as pltpu

F32 = jnp.float32
BF = jnp.bfloat16

D_MODEL = 2048
BATCH = 32
SEQ = 256
DEPTH = 4
DEC_BATCH = 4
DEC_SEQ = 1024
PAST_LEN = 256
GRID_W = 64
N_HEADS = 8
N_KV_HEADS = 2
HEAD_DIM = 128
AXIS_DIM = HEAD_DIM // 2
ROPE_THETA = 10000.0
D_ATTN = N_HEADS * HEAD_DIM
D_KV = N_KV_HEADS * HEAD_DIM
D_RNN = D_MODEL // 2
RNN_BLOCK = 128
N_RNN_BLOCKS = D_RNN // RNN_BLOCK
RNN_CONV_W = 4
LRU_C = 8.0
D_SSM = D_MODEL // 2
SSM_GROUP = 16
N_SSM_GROUPS = D_SSM // SSM_GROUP
SSM_STATE = 64
D_FF = 2 * D_MODEL
N_MOD = 6
EPS = 1e-6
D_IN = D_ATTN + 2 * D_KV + 2 * D_RNN + D_SSM + 3 * D_MODEL

OFF_Q = 0
OFF_K = D_ATTN
OFF_V = OFF_K + D_KV
OFF_XR = OFF_V + D_KV
OFF_XG = OFF_XR + D_RNN
OFF_US = OFF_XG + D_RNN
OFF_GL = OFF_US + D_SSM

T_CTX = BATCH * SEQ
T_LAT = DEC_BATCH * DEC_SEQ
T_ALL = T_CTX + T_LAT
MOD_ROWS = 8

SUBLANES = 8
LANES = 128
VMEM_LIMIT = 56 * 1024 * 1024

S5_SEQS = SUBLANES // 2
S5_GROUPS = LANES // SSM_GROUP
S5_TILES = N_SSM_GROUPS // S5_GROUPS
S5_PAIRS = S5_TILES // 2
S5_STATES = S5_GROUPS * SSM_STATE
S5_CHUNK = 128
S5_GROUP = 32
S5_LAT_ROW0 = (BATCH // S5_SEQS) * SEQ
S5_ROWS = S5_LAT_ROW0 + DEC_SEQ


def _params(n_axes):
    return pltpu.CompilerParams(dimension_semantics=("arbitrary",) * n_axes,
                                vmem_limit_bytes=VMEM_LIMIT)


def _mod_row(first_token):
    return jnp.where(first_token < T_CTX, 0, 1 + jnp.maximum(first_token - T_CTX, 0) // DEC_SEQ)


def _gelu(x):
    return jax.nn.gelu(x)


def _sigmoid_tanh(x):
    return 0.5 * jnp.tanh(0.5 * x) + 0.5


def _mod_kernel(c_ref, w_ref, b_ref, o_ref):
    c = c_ref[...]
    s = (c * jax.nn.sigmoid(c)).astype(BF)
    o_ref[...] = jnp.dot(s, w_ref[...].astype(BF), preferred_element_type=F32) + b_ref[...]


def _modulation(cvec, w_mod, b_mod):
    tn = 1536
    n_out = N_MOD * D_MODEL
    return pl.pallas_call(
        _mod_kernel,
        grid=(DEPTH, n_out // tn),
        in_specs=[pl.BlockSpec((MOD_ROWS, D_MODEL), lambda l, j: (0, 0)),
                  pl.BlockSpec((None, D_MODEL, tn), lambda l, j: (l, 0, j)),
                  pl.BlockSpec((None, 1, tn), lambda l, j: (l, 0, j))],
        out_specs=pl.BlockSpec((None, MOD_ROWS, tn), lambda l, j: (l, 0, j)),
        out_shape=jax.ShapeDtypeStruct((DEPTH, MOD_ROWS, n_out), F32),
        compiler_params=_params(2),
        name="modulation",
    )(cvec, w_mod, b_mod.reshape(DEPTH, 1, n_out))


def _norm_kernel(x_ref, g_ref, sc_ref, sh_ref, o_ref, *, tm):
    r = _mod_row(pl.program_id(0) * tm)
    x = x_ref[...]
    y = x * lax.rsqrt(jnp.mean(x * x, axis=-1, keepdims=True) + EPS)
    y = y * g_ref[...]
    sc = sc_ref[pl.ds(r, 1), :]
    sh = sh_ref[pl.ds(r, 1), :]
    o_ref[...] = (y * (1.0 + sc) + sh).astype(BF)


def _norm_mod(x, gain, mod, layer, shift_idx, scale_idx):
    tm = 512
    return pl.pallas_call(
        functools.partial(_norm_kernel, tm=tm),
        grid=(T_ALL // tm,),
        in_specs=[pl.BlockSpec((tm, D_MODEL), lambda i: (i, 0)),
                  pl.BlockSpec((None, 1, D_MODEL), lambda i: (layer, 0, 0)),
                  pl.BlockSpec((None, MOD_ROWS, D_MODEL), lambda i: (layer, 0, scale_idx)),
                  pl.BlockSpec((None, MOD_ROWS, D_MODEL), lambda i: (layer, 0, shift_idx))],
        out_specs=pl.BlockSpec((tm, D_MODEL), lambda i: (i, 0)),
        out_shape=jax.ShapeDtypeStruct((T_ALL, D_MODEL), BF),
        compiler_params=_params(1),
        name="norm_mod",
    )(x, gain.reshape(DEPTH, 1, D_MODEL), mod, mod)


def _ws_kernel(*refs, n_a, n_w, n_extra, n_out, lhs_of, epilogue, tm):
    a_refs = refs[:n_a]
    w_refs = refs[n_a:n_a + n_w]
    extra = refs[n_a + n_w:n_a + n_w + n_extra]
    outs = refs[n_a + n_w + n_extra:n_a + n_w + n_extra + n_out]
    wbf = refs[n_a + n_w + n_extra + n_out:]
    j = pl.program_id(0)
    i = pl.program_id(1)

    @pl.when(i == 0)
    def _():
        for k in range(n_w):
            wbf[k][...] = w_refs[k][...].astype(BF)

    accs = [jnp.dot(a_refs[lhs_of[k]][...], wbf[k][...], preferred_element_type=F32)
            for k in range(n_w)]
    epilogue(accs, extra, outs, j, i * tm)


def _ws_matmul(a_list, w_list, extra, out_shapes, out_specs, epilogue, *, lhs_of, tm, tn, n_tiles, name):
    in_specs, args, scratch = [], [], []
    w_mode = {}
    for a in a_list:
        in_specs.append(pl.BlockSpec((tm, a.shape[1]), lambda j, i: (i, 0)))
        args.append(a)
    for w, layer, off in w_list:
        kdim = w.shape[1]
        in_specs.append(pl.BlockSpec((None, kdim, tn), lambda j, i, layer=layer, off=off: (layer, 0, off + j),
                                     **w_mode))
        args.append(w)
        scratch.append(pltpu.VMEM((kdim, tn), BF))
    for arr, spec in extra:
        in_specs.append(spec)
        args.append(arr)
    kern = functools.partial(_ws_kernel, n_a=len(a_list), n_w=len(w_list), n_extra=len(extra),
                             n_out=len(out_shapes), lhs_of=lhs_of, epilogue=epilogue, tm=tm)
    return pl.pallas_call(
        kern,
        grid=(n_tiles, T_ALL // tm),
        in_specs=in_specs,
        out_specs=out_specs,
        out_shape=out_shapes,
        scratch_shapes=scratch,
        compiler_params=_params(2),
        name=name,
    )(*args)


def _segment_block(i):
    il = i - BATCH
    per_lat = DEC_SEQ // SEQ
    slot = jnp.where(i < BATCH, i % S5_SEQS, il // per_lat)
    rblk = jnp.where(i < BATCH, i // S5_SEQS, S5_LAT_ROW0 // SEQ + il % per_lat)
    return slot, rblk


def _store_acc(accs, extra, outs, j, row0):
    outs[0][...] = accs[0]


def _in_proj(h, w_in, layer):
    tm = 1024
    tn_a = 896
    z_a = _ws_matmul([h], [(w_in, layer, 0)], [],
                     [jax.ShapeDtypeStruct((T_ALL, OFF_US), F32)],
                     [pl.BlockSpec((tm, tn_a), lambda j, i: (i, j))],
                     _store_acc, lhs_of=(0,), tm=tm, tn=tn_a, n_tiles=OFF_US // tn_a, name="in_proj_a")[0]
    tn_g = 1536

    def gate_epilogue(accs, extra, outs, j, row0):
        outs[0][...] = jax.nn.sigmoid(accs[0]).astype(BF)

    z_g = _ws_matmul([h], [(w_in, layer, OFF_GL // tn_g)], [],
                     [jax.ShapeDtypeStruct((T_ALL, 3 * D_MODEL), BF)],
                     [pl.BlockSpec((tm, tn_g), lambda j, i: (i, j))],
                     gate_epilogue, lhs_of=(0,), tm=tm, tn=tn_g, n_tiles=3 * D_MODEL // tn_g,
                     name="in_proj_g")[0]

    tn_u = 512

    def us_index(j, i):
        slot, rblk = _segment_block(i)
        return slot, rblk, 0

    def us_epilogue(accs, extra, outs, j, row0):
        outs[0][...] = jnp.concatenate(accs, axis=1)

    us = _ws_matmul([h], [(w_in, layer, OFF_US // tn_u), (w_in, layer, OFF_US // tn_u + 1)], [],
                    [jax.ShapeDtypeStruct((S5_SEQS, S5_ROWS, D_SSM), F32)],
                    [pl.BlockSpec((None, SEQ, D_SSM), us_index)],
                    us_epilogue, lhs_of=(0, 0), tm=SEQ, tn=tn_u, n_tiles=1, name="in_proj_u")[0]
    return z_a, z_g, us


def _merge_out_kernel(attn_ref, rec_ref, s5_ref, g_ref, wglu_ref, bglu_ref, wpa_ref, wpr_ref, wps_ref, wo_ref,
                      x_ref, gate_ref, gain_ref, sc_ref, sh_ref, x_out, h_out, *, tm):
    r = _mod_row(pl.program_id(0) * tm)
    s5 = s5_ref[...]
    glu = jnp.dot(s5.astype(BF), wglu_ref[...], preferred_element_type=F32) + bglu_ref[...]
    ssm = (s5 * jax.nn.sigmoid(glu)).astype(BF)
    m = g_ref[:, 0:D_MODEL].astype(F32) * jnp.dot(attn_ref[...], wpa_ref[...], preferred_element_type=F32)
    m = m + (g_ref[:, D_MODEL:2 * D_MODEL].astype(F32)
             * jnp.dot(rec_ref[...], wpr_ref[...], preferred_element_type=F32))
    m = m + (g_ref[:, 2 * D_MODEL:3 * D_MODEL].astype(F32)
             * jnp.dot(ssm, wps_ref[...], preferred_element_type=F32))
    acc = jnp.dot(m.astype(BF), wo_ref[...], preferred_element_type=F32)
    x = x_ref[...] + gate_ref[pl.ds(r, 1), :] * acc
    x_out[...] = x
    y = x * lax.rsqrt(jnp.mean(x * x, axis=-1, keepdims=True) + EPS)
    y = y * gain_ref[...]
    h_out[...] = (y * (1.0 + sc_ref[pl.ds(r, 1), :]) + sh_ref[pl.ds(r, 1), :]).astype(BF)


def _merge_out(attn, rec, s5, gates, glu_w, glu_b, w_pa, w_pr, w_ps, w_o, x, mod, g_ffn, layer):
    tm = SEQ
    row = pl.BlockSpec((tm, D_MODEL), lambda i: (i, 0))
    branch = pl.BlockSpec((tm, D_ATTN), lambda i: (i, 0))
    resident = dict(pipeline_mode=pl.Buffered(1))
    wbr = pl.BlockSpec((None, D_ATTN, D_MODEL), lambda i: (layer, 0, 0), **resident)

    def mod_spec(idx):
        return pl.BlockSpec((None, MOD_ROWS, D_MODEL), lambda i: (layer, 0, idx))

    def seg_index(i):
        slot, rblk = _segment_block(i)
        return slot, rblk, 0

    return pl.pallas_call(
        functools.partial(_merge_out_kernel, tm=tm),
        grid=(T_ALL // tm,),
        in_specs=[branch, branch,
                  pl.BlockSpec((None, SEQ, D_SSM), seg_index),
                  pl.BlockSpec((tm, 3 * D_MODEL), lambda i: (i, 0)),
                  pl.BlockSpec((None, D_SSM, D_SSM), lambda i: (layer, 0, 0), **resident),
                  pl.BlockSpec((None, 1, D_SSM), lambda i: (layer, 0, 0)),
                  wbr, wbr, wbr,
                  pl.BlockSpec((None, D_MODEL, D_MODEL), lambda i: (layer, 0, 0), **resident),
                  row, mod_spec(2),
                  pl.BlockSpec((None, 1, D_MODEL), lambda i: (layer, 0, 0)), mod_spec(4), mod_spec(3)],
        out_specs=[row, row],
        out_shape=[jax.ShapeDtypeStruct((T_ALL, D_MODEL), F32), jax.ShapeDtypeStruct((T_ALL, D_MODEL), BF)],
        compiler_params=_params(1),
        name="merge_out",
    )(attn, rec, s5, gates, glu_w, glu_b.reshape(DEPTH, 1, D_SSM), w_pa, w_pr, w_ps, w_o, x, mod,
      g_ffn.reshape(DEPTH, 1, D_MODEL), mod, mod)


def _residual_kernel(a_ref, w_ref, x_ref, gate_ref, *rest, tm, with_norm):
    r = _mod_row(pl.program_id(0) * tm)
    acc = jnp.dot(a_ref[...], w_ref[...], preferred_element_type=F32)
    x = x_ref[...] + gate_ref[pl.ds(r, 1), :] * acc
    if not with_norm:
        rest[0][...] = x
        return
    g_ref, sc_ref, sh_ref, x_out, h_out = rest
    x_out[...] = x
    y = x * lax.rsqrt(jnp.mean(x * x, axis=-1, keepdims=True) + EPS)
    y = y * g_ref[...]
    h_out[...] = (y * (1.0 + sc_ref[pl.ds(r, 1), :]) + sh_ref[pl.ds(r, 1), :]).astype(BF)


def _residual_proj(a, w_bf, x, mod, layer, gate_idx, name, *, tm, norm=None):
    kdim = a.shape[1]
    row = pl.BlockSpec((tm, D_MODEL), lambda i: (i, 0))
    in_specs = [pl.BlockSpec((tm, kdim), lambda i: (i, 0)),
                pl.BlockSpec((None, kdim, D_MODEL), lambda i: (layer, 0, 0), pipeline_mode=pl.Buffered(1)),
                row,
                pl.BlockSpec((None, MOD_ROWS, D_MODEL), lambda i: (layer, 0, gate_idx))]
    args = [a, w_bf, x, mod]
    out_shape = [jax.ShapeDtypeStruct((T_ALL, D_MODEL), F32)]
    out_specs = [row]
    if norm is not None:
        gain, nl, shift_idx, scale_idx = norm
        in_specs += [pl.BlockSpec((None, 1, D_MODEL), lambda i: (nl, 0, 0)),
                     pl.BlockSpec((None, MOD_ROWS, D_MODEL), lambda i: (nl, 0, scale_idx)),
                     pl.BlockSpec((None, MOD_ROWS, D_MODEL), lambda i: (nl, 0, shift_idx))]
        args += [gain.reshape(DEPTH, 1, D_MODEL), mod, mod]
        out_shape.append(jax.ShapeDtypeStruct((T_ALL, D_MODEL), BF))
        out_specs.append(row)
    return pl.pallas_call(
        functools.partial(_residual_kernel, tm=tm, with_norm=norm is not None),
        grid=(T_ALL // tm,),
        in_specs=in_specs,
        out_specs=out_specs,
        out_shape=out_shape,
        compiler_params=_params(1),
        name=name,
    )(*args)


def _ffn_up(h2, w_up, conv_w, conv_b, layer):
    tm, tn = DEC_SEQ, 512

    def epilogue(accs, extra, outs, j, row0):
        a, b = accs
        seq_len = jnp.where(row0 < T_CTX, SEQ, DEC_SEQ)
        pos = lax.broadcasted_iota(jnp.int32, a.shape, 0) & (seq_len - 1)
        prev = jnp.where(pos == 0, 0.0, pltpu.roll(a, 1, 0))
        nxt = jnp.where(pos == seq_len - 1, 0.0, pltpu.roll(a, tm - 1, 0))
        cw = extra[0][...]
        a = cw[0:1] * prev + cw[1:2] * a + cw[2:3] * nxt + extra[1][...]
        outs[0][...] = (_gelu(a) * b).astype(BF)

    extra = [(conv_w, pl.BlockSpec((None, 3, tn), lambda j, i: (layer, 0, j))),
             (conv_b.reshape(DEPTH, 1, D_FF), pl.BlockSpec((None, 1, tn), lambda j, i: (layer, 0, j)))]
    return _ws_matmul([h2], [(w_up, layer, 0), (w_up, layer, D_FF // tn)], extra,
                      [jax.ShapeDtypeStruct((T_ALL, D_FF), BF)],
                      [pl.BlockSpec((tm, tn), lambda j, i: (i, j))],
                      epilogue, lhs_of=(0, 0), tm=tm, tn=tn, n_tiles=D_FF // tn, name="ffn_up")[0]


def _head_norm(x, gain):
    return x * lax.rsqrt(jnp.mean(x * x, axis=-1, keepdims=True) + EPS) * gain


def _rope(x, cos, sin_signed):
    lane = lax.broadcasted_iota(jnp.int32, x.shape, 1)
    first_half = (lane & (AXIS_DIM - 1)) < AXIS_DIM // 2
    partner = jnp.where(first_half, pltpu.roll(x, HEAD_DIM - AXIS_DIM // 2, 1),
                        pltpu.roll(x, AXIS_DIM // 2, 1))
    return x * cos + partner * sin_signed


def _scores(q, k):
    return lax.dot_general(q, k, (((1,), (1,)), ((), ())), preferred_element_type=F32) * (HEAD_DIM ** -0.5)


def _attn_ctx_kernel(q_ref, k_ref, v_ref, qg_ref, kg_ref, keep_a, keep_k, keep_v, o_ref, ko_ref, vo_ref):
    del keep_a, keep_k, keep_v
    grp = N_HEADS // N_KV_HEADS
    vo_ref[...] = v_ref[...]
    kn, vv = [], []
    for kv in range(N_KV_HEADS):
        ks = slice(kv * HEAD_DIM, (kv + 1) * HEAD_DIM)
        k = _head_norm(k_ref[:, ks], kg_ref[...])
        ko_ref[:, ks] = k
        kn.append(k.astype(BF))
        vv.append(v_ref[:, ks].astype(BF))

    def head_scores(h):
        qn = _head_norm(q_ref[:, h * HEAD_DIM:(h + 1) * HEAD_DIM], qg_ref[...]).astype(BF)
        return _scores(qn, kn[h // grp])

    s_next = head_scores(0)
    for h in range(N_HEADS):
        s = s_next
        if h + 1 < N_HEADS:
            s_next = head_scores(h + 1)
        p = jnp.exp(s - jnp.max(s, axis=-1, keepdims=True))
        den = jnp.sum(p, axis=-1, keepdims=True)
        o = jnp.dot(p.astype(BF), vv[h // grp], preferred_element_type=F32) / den
        o_ref[:, h * HEAD_DIM:(h + 1) * HEAD_DIM] = o.astype(BF)


def _attn_lat_kernel(q_ref, k_ref, v_ref, ck_ref, cv_ref, cosq_ref, sinq_ref, cosk_ref, sink_ref,
                     qg_ref, kg_ref, filled_ref, o_ref):
    del filled_ref
    grp = N_HEADS // N_KV_HEADS
    kn, vv, ck, cv = [], [], [], []
    for kv in range(N_KV_HEADS):
        ks = slice(kv * HEAD_DIM, (kv + 1) * HEAD_DIM)
        kn.append(_rope(_head_norm(k_ref[:, ks], kg_ref[...]), cosk_ref[...], sink_ref[...]).astype(BF))
        vv.append(v_ref[:, ks].astype(BF))
        ck.append(ck_ref[:, ks].astype(BF))
        cv.append(cv_ref[:, ks].astype(BF))

    def head_scores(h):
        q = _head_norm(q_ref[:, h * HEAD_DIM:(h + 1) * HEAD_DIM], qg_ref[...])
        qn = _rope(q, cosq_ref[...], sinq_ref[...]).astype(BF)
        return _scores(qn, ck[h // grp]), _scores(qn, kn[h // grp])

    s_next = head_scores(0)
    for h in range(N_HEADS):
        s1, s2 = s_next
        if h + 1 < N_HEADS:
            s_next = head_scores(h + 1)
        m = jnp.maximum(jnp.max(s1, axis=-1, keepdims=True), jnp.max(s2, axis=-1, keepdims=True))
        p1 = jnp.exp(s1 - m)
        p2 = jnp.exp(s2 - m)
        den = jnp.sum(p1, axis=-1, keepdims=True) + jnp.sum(p2, axis=-1, keepdims=True)
        o = (jnp.dot(p1.astype(BF), cv[h // grp], preferred_element_type=F32)
             + jnp.dot(p2.astype(BF), vv[h // grp], preferred_element_type=F32)) / den
        o_ref[:, h * HEAD_DIM:(h + 1) * HEAD_DIM] = o.astype(BF)


def _attention(z, q_gain, k_gain, cache_k, cache_v, cos_t, sin_t, layer, attn_buf, new_k, new_v):
    gains = [pl.BlockSpec((None, 1, HEAD_DIM), lambda *_: (layer, 0, 0))] * 2
    qg = q_gain.reshape(DEPTH, 1, HEAD_DIM)
    kg = k_gain.reshape(DEPTH, 1, HEAD_DIM)
    kcol, vcol = OFF_K // D_KV, OFF_V // D_KV
    keep = pl.BlockSpec(memory_space=pl.ANY)
    cache_spec = pl.BlockSpec((None, None, SEQ, D_KV), lambda b: (b, layer, 0, 0))
    attn, new_k, new_v = pl.pallas_call(
        _attn_ctx_kernel,
        grid=(BATCH,),
        in_specs=[pl.BlockSpec((SEQ, D_ATTN), lambda b: (b, 0)),
                  pl.BlockSpec((SEQ, D_KV), lambda b: (b, kcol)),
                  pl.BlockSpec((SEQ, D_KV), lambda b: (b, vcol))] + gains + [keep, keep, keep],
        out_specs=[pl.BlockSpec((SEQ, D_ATTN), lambda b: (b, 0)), cache_spec, cache_spec],
        out_shape=[jax.ShapeDtypeStruct((T_ALL, D_ATTN), BF),
                   jax.ShapeDtypeStruct((BATCH, DEPTH, SEQ, D_KV), F32),
                   jax.ShapeDtypeStruct((BATCH, DEPTH, SEQ, D_KV), F32)],
        input_output_aliases={5: 0, 6: 1, 7: 2},
        compiler_params=_params(1),
        name="attn_ctx",
    )(z, z, z, qg, kg, attn_buf, new_k, new_v)

    tq = 512
    nq = DEC_SEQ // tq
    lat0 = T_CTX // DEC_SEQ
    attn = pl.pallas_call(
        _attn_lat_kernel,
        grid=(DEC_BATCH, nq),
        in_specs=[pl.BlockSpec((tq, D_ATTN), lambda b, i: ((lat0 + b) * nq + i, 0)),
                  pl.BlockSpec((DEC_SEQ, D_KV), lambda b, i: (lat0 + b, kcol)),
                  pl.BlockSpec((DEC_SEQ, D_KV), lambda b, i: (lat0 + b, vcol)),
                  pl.BlockSpec((None, None, PAST_LEN, D_KV), lambda b, i: (b, layer, 0, 0)),
                  pl.BlockSpec((None, None, PAST_LEN, D_KV), lambda b, i: (b, layer, 0, 0)),
                  pl.BlockSpec((tq, HEAD_DIM), lambda b, i: (i, 0)),
                  pl.BlockSpec((tq, HEAD_DIM), lambda b, i: (i, 0)),
                  pl.BlockSpec((DEC_SEQ, HEAD_DIM), lambda b, i: (0, 0)),
                  pl.BlockSpec((DEC_SEQ, HEAD_DIM), lambda b, i: (0, 0))] + gains
                 + [pl.BlockSpec(memory_space=pl.ANY)],
        out_specs=pl.BlockSpec((tq, D_ATTN), lambda b, i: ((lat0 + b) * nq + i, 0)),
        out_shape=jax.ShapeDtypeStruct((T_ALL, D_ATTN), BF),
        input_output_aliases={11: 0},
        compiler_params=_params(2),
        name="attn_lat",
    )(z, z, z, cache_k, cache_v, cos_t, sin_t, cos_t, sin_t, qg, kg, attn)
    return attn, new_k, new_v


def _lru_kernel(xr_ref, xg_ref, h0_ref, cw_ref, cb_ref, gw_ref, gb_ref, lam_ref, *rest, seq, ct):
    o_ref, hf_ref, a_s, b_s, hfw_s, hbw_s = rest[-6:]
    n_blocks = ct // RNN_BLOCK
    n_tiles = seq // SUBLANES
    row = lax.broadcasted_iota(jnp.int32, (seq, ct), 0)
    xr = xr_ref[...]

    def shifted(x, k):
        if k > 0:
            return jnp.where(row >= k, pltpu.roll(x, k, 0), 0.0)
        return jnp.where(row < seq + k, pltpu.roll(x, seq + k, 0), 0.0)

    cw = cw_ref[...]
    xc = (cb_ref[...] + cw[0:1] * shifted(xr, 2) + cw[1:2] * shifted(xr, 1) + cw[2:3] * xr
          + cw[3:4] * shifted(xr, -1))

    for kb in range(n_blocks):
        cs = slice(kb * RNN_BLOCK, (kb + 1) * RNN_BLOCK)
        xcb = xc[:, cs]
        gates = jnp.dot(xcb.astype(BF), gw_ref[kb], preferred_element_type=F32) + gb_ref[kb]
        for d in range(2):
            a, b = _lru_gate_terms(xcb, gates, lam_ref, kb, d)
            a_s[d, :, cs] = a
            b_s[d, :, cs] = b

    sub = lax.broadcasted_iota(jnp.int32, (SUBLANES, ct), 0)

    def tile_scan(a, b, forward):
        for k in (1, 2, 4):
            if forward:
                keep = sub >= k
                shift = k
            else:
                keep = sub < SUBLANES - k
                shift = SUBLANES - k
            a_n = jnp.where(keep, pltpu.roll(a, shift, 0), 1.0)
            b_n = jnp.where(keep, pltpu.roll(b, shift, 0), 0.0)
            b = a * b_n + b
            a = a * a_n
        return a, b

    def body(j, carry):
        hf, hb = carry
        r0 = pl.multiple_of(j * SUBLANES, SUBLANES)
        a, b = tile_scan(a_s[0, pl.ds(r0, SUBLANES), :], b_s[0, pl.ds(r0, SUBLANES), :], True)
        h = a * hf + b
        hfw_s[pl.ds(r0, SUBLANES), :] = h
        hf = h[SUBLANES - 1:SUBLANES, :]
        r1 = pl.multiple_of((n_tiles - 1 - j) * SUBLANES, SUBLANES)
        a, b = tile_scan(a_s[1, pl.ds(r1, SUBLANES), :], b_s[1, pl.ds(r1, SUBLANES), :], False)
        h = a * hb + b
        hbw_s[pl.ds(r1, SUBLANES), :] = h
        hb = h[0:1, :]
        return hf, hb

    hf, hb = lax.fori_loop(0, n_tiles, body, (h0_ref[0:1, :], h0_ref[1:2, :]))
    hf_ref[0:1, :] = hf
    hf_ref[1:2, :] = hb
    o_ref[...] = (_gelu(xg_ref[...]) * (hfw_s[...] + hbw_s[...])).astype(BF)


def _lru_call(z, h0, cw, cb, gw, gb, lam, layer, *, n_seq, seq, tok0, ct, filled=None):
    nc = D_RNN // ct
    s0 = tok0 // seq
    blk = ct // RNN_BLOCK
    alias = {} if filled is None else dict(input_output_aliases={8: 0})
    alias_spec = [] if filled is None else [pl.BlockSpec(memory_space=pl.ANY)]
    alias_arg = [] if filled is None else [filled]
    return pl.pallas_call(
        functools.partial(_lru_kernel, seq=seq, ct=ct),
        grid=(n_seq, nc),
        in_specs=[pl.BlockSpec((seq, ct), lambda s, c: (s0 + s, OFF_XR // ct + c)),
                  pl.BlockSpec((seq, ct), lambda s, c: (s0 + s, OFF_XG // ct + c)),
                  pl.BlockSpec((None, None, 2, ct), lambda s, c: (s, layer, 0, c)),
                  pl.BlockSpec((None, RNN_CONV_W, ct), lambda s, c: (layer, 0, c)),
                  pl.BlockSpec((None, 1, ct), lambda s, c: (layer, 0, c)),
                  pl.BlockSpec((None, blk, RNN_BLOCK, 4 * RNN_BLOCK), lambda s, c: (layer, c, 0, 0)),
                  pl.BlockSpec((None, blk, 1, 4 * RNN_BLOCK), lambda s, c: (layer, c, 0, 0)),
                  pl.BlockSpec((None, blk, 2, RNN_BLOCK), lambda s, c: (layer, c, 0, 0))] + alias_spec,
        out_specs=[pl.BlockSpec((seq, ct), lambda s, c: (s0 + s, c)),
                   pl.BlockSpec((None, 2, ct), lambda s, c: (s, 0, c))],
        out_shape=[jax.ShapeDtypeStruct((T_ALL, D_RNN), BF),
                   jax.ShapeDtypeStruct((n_seq, 2, D_RNN), F32)],
        scratch_shapes=[pltpu.VMEM((2, seq, ct), F32), pltpu.VMEM((2, seq, ct), F32),
                        pltpu.VMEM((seq, ct), F32), pltpu.VMEM((seq, ct), F32)],
        compiler_params=_params(2),
        name="rg_lru",
        **alias,
    )(z, z, h0, cw, cb, gw, gb, lam, *alias_arg)


def _lru_gate_terms(xc, gates, lam_ref, kb, d):
    base = d * 2 * RNN_BLOCK
    r = _sigmoid_tanh(gates[:, base:base + RNN_BLOCK])
    ig = _sigmoid_tanh(gates[:, base + RNN_BLOCK:base + 2 * RNN_BLOCK])
    lam = lam_ref[kb, d:d + 1, :]
    a = jnp.exp(r * (-LRU_C * jnp.log1p(jnp.exp(-lam))))
    om = 1.0 - a * a
    root = jnp.where(om > 0.0, om * lax.rsqrt(om), 0.0)
    return a, root * (ig * xc)


LRU_CTX_SEQS = SUBLANES
LRU_PAD_FRONT = (RNN_CONV_W // 2) * SUBLANES
LRU_PAD_BACK = (RNN_CONV_W - 1 - RNN_CONV_W // 2) * SUBLANES


def _lru_ctx_kernel(xr_ref, xg_ref, cw_ref, cb_ref, gw_ref, gb_ref, lam_ref, keep_ref, o_ref, hf_ref,
                    xs, a_s, b_s, hfw, hbw, *, ct):
    del keep_ref
    n_blocks = ct // RNN_BLOCK
    rows = SEQ * SUBLANES

    @pl.when((pl.program_id(0) == 0) & (pl.program_id(1) == 0))
    def _():
        xs[...] = jnp.zeros_like(xs)

    for kb in range(n_blocks):
        cs = slice(kb * RNN_BLOCK, (kb + 1) * RNN_BLOCK)
        for b in range(LRU_CTX_SEQS):
            xs[kb, pl.ds(LRU_PAD_FRONT + b, SEQ, stride=SUBLANES), :] = xr_ref[b, :, cs]

    for kb in range(n_blocks):
        cs = slice(kb * RNN_BLOCK, (kb + 1) * RNN_BLOCK)
        xc = cb_ref[:, cs]
        for j in range(RNN_CONV_W):
            xc = xc + cw_ref[j:j + 1, cs] * xs[kb, j * SUBLANES:j * SUBLANES + rows, :]
        gates = jnp.dot(xc.astype(BF), gw_ref[kb], preferred_element_type=F32) + gb_ref[kb]
        for d in range(2):
            a, b = _lru_gate_terms(xc, gates, lam_ref, kb, d)
            a_s[d, kb] = a
            b_s[d, kb] = b

    def body(t, carry):
        hf, hb = carry
        r0 = pl.multiple_of(t * SUBLANES, SUBLANES)
        r1 = pl.multiple_of((SEQ - 1 - t) * SUBLANES, SUBLANES)
        new_f, new_b = [], []
        for kb in range(n_blocks):
            f = a_s[0, kb, pl.ds(r0, SUBLANES), :] * hf[kb] + b_s[0, kb, pl.ds(r0, SUBLANES), :]
            hfw[kb, pl.ds(r0, SUBLANES), :] = f
            new_f.append(f)
            g = a_s[1, kb, pl.ds(r1, SUBLANES), :] * hb[kb] + b_s[1, kb, pl.ds(r1, SUBLANES), :]
            hbw[kb, pl.ds(r1, SUBLANES), :] = g
            new_b.append(g)
        return tuple(new_f), tuple(new_b)

    zero = tuple(jnp.zeros((SUBLANES, RNN_BLOCK), F32) for _ in range(n_blocks))
    hf, hb = lax.fori_loop(0, SEQ, body, (zero, zero), unroll=8)
    for kb in range(n_blocks):
        cs = slice(kb * RNN_BLOCK, (kb + 1) * RNN_BLOCK)
        hf_ref[0, :, cs] = hf[kb]
        hf_ref[1, :, cs] = hb[kb]
        for b in range(LRU_CTX_SEQS):
            h = (hfw[kb, pl.ds(b, SEQ, stride=SUBLANES), :] + hbw[kb, pl.ds(b, SEQ, stride=SUBLANES), :])
            o_ref[b, :, cs] = (_gelu(xg_ref[b, :, cs]) * h).astype(BF)


def _lru_ctx_call(z, cw, cb, gw, gb, lam, layer, filled):
    ct = 256
    blk = ct // RNN_BLOCK
    n_seg = T_ALL // SEQ
    z3 = z.reshape(n_seg, SEQ, z.shape[1])
    rows = SEQ * SUBLANES
    pad_rows = LRU_PAD_FRONT + rows + LRU_PAD_BACK
    tok = pl.BlockSpec((LRU_CTX_SEQS, SEQ, ct), lambda s, c: (s, 0, c))
    rec, fin = pl.pallas_call(
        functools.partial(_lru_ctx_kernel, ct=ct),
        grid=(BATCH // LRU_CTX_SEQS, D_RNN // ct),
        in_specs=[pl.BlockSpec((LRU_CTX_SEQS, SEQ, ct), lambda s, c: (s, 0, OFF_XR // ct + c)),
                  pl.BlockSpec((LRU_CTX_SEQS, SEQ, ct), lambda s, c: (s, 0, OFF_XG // ct + c)),
                  pl.BlockSpec((None, RNN_CONV_W, ct), lambda s, c: (layer, 0, c)),
                  pl.BlockSpec((None, 1, ct), lambda s, c: (layer, 0, c)),
                  pl.BlockSpec((None, blk, RNN_BLOCK, 4 * RNN_BLOCK), lambda s, c: (layer, c, 0, 0)),
                  pl.BlockSpec((None, blk, 1, 4 * RNN_BLOCK), lambda s, c: (layer, c, 0, 0)),
                  pl.BlockSpec((None, blk, 2, RNN_BLOCK), lambda s, c: (layer, c, 0, 0)),
                  pl.BlockSpec(memory_space=pl.ANY)],
        out_specs=[tok, pl.BlockSpec((None, 2, LRU_CTX_SEQS, ct), lambda s, c: (s, 0, 0, c))],
        out_shape=[jax.ShapeDtypeStruct((n_seg, SEQ, D_RNN), BF),
                   jax.ShapeDtypeStruct((BATCH // LRU_CTX_SEQS, 2, LRU_CTX_SEQS, D_RNN), F32)],
        scratch_shapes=[pltpu.VMEM((blk, pad_rows, RNN_BLOCK), F32),
                        pltpu.VMEM((2, blk, rows, RNN_BLOCK), F32), pltpu.VMEM((2, blk, rows, RNN_BLOCK), F32),
                        pltpu.VMEM((blk, rows, RNN_BLOCK), F32), pltpu.VMEM((blk, rows, RNN_BLOCK), F32)],
        input_output_aliases={7: 0},
        compiler_params=_params(2),
        name="rg_lru_ctx",
    )(z3, z3, cw, cb, gw, gb, lam, filled.reshape(n_seg, SEQ, D_RNN))
    return rec.reshape(T_ALL, D_RNN), fin


def _s5_disc_kernel(are_ref, aim_ref, ldt_ref, bre_ref, bim_ref, abr_ref, abi_ref, bbr_ref, bbi_ref):
    lam_r = jnp.minimum(are_ref[...], -1e-4)
    lam_i = aim_ref[...]
    dt = jnp.exp(ldt_ref[...])
    mag = jnp.exp(lam_r * dt)
    ab_r = mag * jnp.cos(lam_i * dt)
    ab_i = mag * jnp.sin(lam_i * dt)
    den = lam_r * lam_r + lam_i * lam_i
    q_r = ((ab_r - 1.0) * lam_r + ab_i * lam_i) / den
    q_i = (ab_i * lam_r - (ab_r - 1.0) * lam_i) / den
    abr_ref[...] = ab_r
    abi_ref[...] = ab_i
    br = bre_ref[...]
    bi = bim_ref[...]
    bbr_ref[...] = q_r[:, None, :] * br - q_i[:, None, :] * bi
    bbi_ref[...] = q_r[:, None, :] * bi + q_i[:, None, :] * br


def _s5_discretize(a_re, a_im, log_dt, b_re, b_im):
    n = DEPTH * 2 * N_SSM_GROUPS
    bt_re = jnp.swapaxes(b_re, -1, -2).reshape(n, SSM_GROUP, SSM_STATE)
    bt_im = jnp.swapaxes(b_im, -1, -2).reshape(n, SSM_GROUP, SSM_STATE)
    vec = jax.ShapeDtypeStruct((n, SSM_STATE), F32)
    mat = jax.ShapeDtypeStruct((n, SSM_GROUP, SSM_STATE), F32)
    return pl.pallas_call(
        _s5_disc_kernel,
        out_shape=[vec, vec, mat, mat],
        compiler_params=pltpu.CompilerParams(vmem_limit_bytes=VMEM_LIMIT),
        name="s5_discretize",
    )(a_re.reshape(n, SSM_STATE), a_im.reshape(n, SSM_STATE), log_dt.reshape(n, 1), bt_re, bt_im)


def _s5_kernel(u_ref, h0r_ref, h0i_ref, ar_ref, ai_ref, wb_ref, wcr_ref, wci_ref, d_ref, *rest, seq):
    y_ref, fr_ref, fi_ref, lhs_a, lhs_b, acc_s = rest[-6:]
    rows = S5_CHUNK * SUBLANES
    n_chunks = seq // S5_CHUNK

    @pl.when((pl.program_id(0) == 0) & (pl.program_id(1) == 0))
    def _():
        lhs_a[...] = jnp.zeros_like(lhs_a)
        lhs_b[...] = jnp.zeros_like(lhs_b)

    for b in range(S5_SEQS):
        lhs_a[pl.ds(b, seq, stride=SUBLANES), :] = u_ref[b, :, 0:LANES]
        lhs_b[pl.ds(S5_SEQS + b, seq, stride=SUBLANES), :] = u_ref[b, :, LANES:2 * LANES]

    grows = S5_GROUP * SUBLANES
    n_groups = S5_CHUNK // S5_GROUP

    def bu_rows(d, r0, n):
        lhs = jnp.concatenate([lhs_a[pl.ds(r0, n), :], lhs_b[pl.ds(r0, n), :]], axis=1).astype(BF)
        return jnp.dot(lhs, wb_ref[d], preferred_element_type=F32)

    def add_output(d, r0, h_re, h_im):
        y = (jnp.dot(h_re.astype(BF), wcr_ref[d], preferred_element_type=F32)
             - jnp.dot(h_im.astype(BF), wci_ref[d], preferred_element_type=F32))
        rid = lax.broadcasted_iota(jnp.int32, (h_re.shape[0], LANES), 0)
        y = jnp.where((rid & (SUBLANES - 1)) < S5_SEQS, y[:, 0:LANES], y[:, LANES:2 * LANES])
        if d == 0:
            acc_s[pl.ds(r0, h_re.shape[0]), :] = y
        else:
            acc_s[pl.ds(r0, h_re.shape[0]), :] += y

    for d in range(2):
        ar = ar_ref[d]
        ai = ai_ref[d]

        def chunk(c, carry, d=d, ar=ar, ai=ai):
            cc = c if d == 0 else n_chunks - 1 - c
            r0 = pl.multiple_of(cc * rows, rows)
            hr, hi = carry
            order = [g if d == 0 else n_groups - 1 - g for g in range(n_groups)]
            bu_next = bu_rows(d, r0 + order[0] * grows, grows)
            for g in range(n_groups):
                q = r0 + order[g] * grows
                bu = bu_next
                if g + 1 < n_groups:
                    bu_next = bu_rows(d, r0 + order[g + 1] * grows, grows)
                hs_r = [None] * S5_GROUP
                hs_i = [None] * S5_GROUP
                for s in range(S5_GROUP):
                    ss = s if d == 0 else S5_GROUP - 1 - s
                    br = bu[ss * SUBLANES:(ss + 1) * SUBLANES, 0:S5_STATES]
                    bi = bu[ss * SUBLANES:(ss + 1) * SUBLANES, S5_STATES:2 * S5_STATES]
                    hr, hi = ar * hr - ai * hi + br, ar * hi + ai * hr + bi
                    hs_r[ss] = hr
                    hs_i[ss] = hi
                add_output(d, q, jnp.concatenate(hs_r, axis=0), jnp.concatenate(hs_i, axis=0))
            return hr, hi

        hr, hi = lax.fori_loop(0, n_chunks, chunk, (h0r_ref[d], h0i_ref[d]))
        fr_ref[d] = hr
        fi_ref[d] = hi

    for b in range(S5_SEQS):
        y = jnp.concatenate([acc_s[pl.ds(b, seq, stride=SUBLANES), :],
                             acc_s[pl.ds(S5_SEQS + b, seq, stride=SUBLANES), :]], axis=1)
        y_ref[b] = _gelu(y + d_ref[...] * u_ref[b])


def _s5_call(us, h0r, h0i, h0_idx, ab_r, ab_i, wb, wcr, wci, d_skip, layer, *, n_blk, seq, row0, filled=None):
    blk0 = row0 // seq
    st = (None, None, 2, SUBLANES, S5_STATES)
    tok_spec = pl.BlockSpec((S5_SEQS, seq, 2 * LANES), lambda b, p: (0, blk0 + b, p))
    h0_spec = pl.BlockSpec(st, lambda b, p: (h0_idx, p, 0, 0, 0))
    par_spec = pl.BlockSpec(st, lambda b, p: (layer, p, 0, 0, 0))
    fin_spec = pl.BlockSpec(st, lambda b, p: (b, p, 0, 0, 0))
    fin_shape = jax.ShapeDtypeStruct((n_blk, S5_PAIRS, 2, SUBLANES, S5_STATES), F32)
    alias = {} if filled is None else dict(input_output_aliases={9: 0})
    alias_spec = [] if filled is None else [pl.BlockSpec(memory_space=pl.ANY)]
    alias_arg = [] if filled is None else [filled]
    return pl.pallas_call(
        functools.partial(_s5_kernel, seq=seq),
        grid=(n_blk, S5_PAIRS),
        in_specs=[tok_spec, h0_spec, h0_spec, par_spec, par_spec,
                  pl.BlockSpec((None, None, 2, 2 * LANES, 2 * S5_STATES), lambda b, p: (layer, p, 0, 0, 0)),
                  pl.BlockSpec((None, None, 2, S5_STATES, 2 * LANES), lambda b, p: (layer, p, 0, 0, 0)),
                  pl.BlockSpec((None, None, 2, S5_STATES, 2 * LANES), lambda b, p: (layer, p, 0, 0, 0)),
                  pl.BlockSpec((None, 1, 2 * LANES), lambda b, p: (layer, 0, p))] + alias_spec,
        out_specs=[tok_spec, fin_spec, fin_spec],
        out_shape=[jax.ShapeDtypeStruct((S5_SEQS, S5_ROWS, D_SSM), F32), fin_shape, fin_shape],
        scratch_shapes=[pltpu.VMEM((seq * SUBLANES, LANES), F32), pltpu.VMEM((seq * SUBLANES, LANES), F32),
                        pltpu.VMEM((seq * SUBLANES, LANES), F32)],
        compiler_params=_params(2),
        name="s5_scan",
        **alias,
    )(us, h0r, h0i, ab_r, ab_i, wb, wcr, wci, d_skip, *alias_arg)


def _s5_pair_rows(st):
    x = st.reshape(S5_SEQS, DEPTH, 2, S5_PAIRS, 2, S5_STATES)
    return x.transpose(1, 3, 2, 4, 0, 5).reshape(DEPTH, S5_PAIRS, 2, SUBLANES, S5_STATES)


def _rope_tables():
    rows = DEC_SEQ // GRID_W
    t_row = jnp.repeat(jnp.arange(rows, dtype=F32), GRID_W)
    t_col = (jnp.arange(DEC_SEQ) % GRID_W).astype(F32)
    inv = 1.0 / (ROPE_THETA ** (jnp.arange(0, AXIS_DIM, 2, dtype=F32) / AXIS_DIM))
    ang_r = t_row[:, None] * inv
    ang_c = t_col[:, None] * inv
    cos = jnp.concatenate([jnp.cos(ang_r), jnp.cos(ang_r), jnp.cos(ang_c), jnp.cos(ang_c)], axis=-1)
    sin = jnp.concatenate([-jnp.sin(ang_r), jnp.sin(ang_r), -jnp.sin(ang_c), jnp.sin(ang_c)], axis=-1)
    return cos, sin


def _block_diag(blocks):
    n, r, c = blocks.shape[-3:]
    eye = jnp.eye(n, dtype=blocks.dtype)
    out = blocks[..., :, :, None, :] * eye[:, None, :, None]
    return out.reshape(blocks.shape[:-3] + (n * r, n * c))


def kernel(x_prompt, x_sample, c, cache_k, cache_v, state_lru, state_s5_re, state_s5_im, c_ctx, w_mod, b_mod, g_mix, w_in, q_gain, k_gain, rnn_conv_w, rnn_conv_b, lru_gate_w, lru_gate_b, lru_lambda, s5_a_re, s5_a_im, s5_log_dt, s5_b_re, s5_b_im, s5_c_re, s5_c_im, s5_d, glu_w, glu_b, w_pa, w_pr, w_ps, w_o, g_ffn, w_up, ffn_conv_w, ffn_conv_b, w_down):
    x = jnp.concatenate([x_prompt.reshape(T_CTX, D_MODEL), x_sample.reshape(T_LAT, D_MODEL)], axis=0)

    cvec = jnp.zeros((MOD_ROWS, D_MODEL), F32).at[0].set(c_ctx).at[1:1 + DEC_BATCH].set(c)
    mod = _modulation(cvec, w_mod, b_mod)
    cos_t, sin_t = _rope_tables()

    gw = lru_gate_w.transpose(0, 3, 4, 1, 2, 5).reshape(DEPTH, N_RNN_BLOCKS, RNN_BLOCK, 4 * RNN_BLOCK).astype(BF)
    gb = (lru_gate_b.reshape(DEPTH, 2, 2, N_RNN_BLOCKS, RNN_BLOCK).transpose(0, 3, 1, 2, 4)
          .reshape(DEPTH, N_RNN_BLOCKS, 1, 4 * RNN_BLOCK))
    lam = lru_lambda.reshape(DEPTH, 2, N_RNN_BLOCKS, RNN_BLOCK).transpose(0, 2, 1, 3)
    cb = rnn_conv_b.reshape(DEPTH, 1, D_RNN)

    ab_r, ab_i, bbt_r, bbt_i = _s5_discretize(s5_a_re, s5_a_im, s5_log_dt, s5_b_re, s5_b_im)
    def pair_rows(ab):
        x = ab.reshape(DEPTH, 2, S5_PAIRS, 2, 1, S5_STATES)
        x = jnp.broadcast_to(x, (DEPTH, 2, S5_PAIRS, 2, S5_SEQS, S5_STATES))
        return x.transpose(0, 2, 1, 3, 4, 5).reshape(DEPTH, S5_PAIRS, 2, SUBLANES, S5_STATES)

    ab_r = pair_rows(ab_r)
    ab_i = pair_rows(ab_i)
    gshape = (DEPTH, 2, S5_TILES, S5_GROUPS, SSM_GROUP, SSM_STATE)
    wb = jnp.concatenate([_block_diag(bbt_r.reshape(gshape)), _block_diag(bbt_i.reshape(gshape))], axis=-1)
    wb = wb.reshape(DEPTH, 2, S5_PAIRS, 2 * LANES, 2 * S5_STATES)
    wb = wb.transpose(0, 2, 1, 3, 4).astype(BF)

    def pair_cols(c):
        x = _block_diag(jnp.swapaxes(c.reshape(gshape), -1, -2))
        x = x.reshape(DEPTH, 2, S5_PAIRS, 2, S5_STATES, LANES)
        return x.transpose(0, 2, 1, 4, 3, 5).reshape(DEPTH, S5_PAIRS, 2, S5_STATES, 2 * LANES).astype(BF)

    wcr = pair_cols(s5_c_re)
    wci = pair_cols(s5_c_im)
    d_skip = s5_d.reshape(DEPTH, 1, D_SSM)

    zeros_s5 = jnp.zeros((1, S5_PAIRS, 2, SUBLANES, S5_STATES), F32)
    cache_k = cache_k.reshape(DEC_BATCH, DEPTH, PAST_LEN, D_KV)
    cache_v = cache_v.reshape(DEC_BATCH, DEPTH, PAST_LEN, D_KV)

    w_pa_bf, w_pr_bf, w_ps_bf = w_pa.astype(BF), w_pr.astype(BF), w_ps.astype(BF)
    w_o_bf = w_o.astype(BF)
    w_down_bf = w_down.astype(BF)
    glu_w_bf = glu_w.astype(BF)
    lat_s5_re = _s5_pair_rows(state_s5_re)
    lat_s5_im = _s5_pair_rows(state_s5_im)

    attn = jnp.zeros((T_ALL, D_ATTN), BF)
    rec = jnp.zeros((T_ALL, D_RNN), BF)
    s5 = jnp.zeros((S5_SEQS, S5_ROWS, D_SSM), F32)
    new_k = jnp.zeros((BATCH, DEPTH, SEQ, D_KV), F32)
    new_v = jnp.zeros((BATCH, DEPTH, SEQ, D_KV), F32)

    new_lru, new_sr, new_si = [], [], []
    h = _norm_mod(x, g_mix, mod, 0, 0, 1)
    for l in range(DEPTH):
        z, gates, us = _in_proj(h, w_in, l)

        attn, new_k, new_v = _attention(z, q_gain, k_gain, cache_k, cache_v, cos_t, sin_t, l,
                                        attn, new_k, new_v)

        rec, lru_f = _lru_ctx_call(z, rnn_conv_w, cb, gw, gb, lam, l, rec)
        rec, _ = _lru_call(z, state_lru, rnn_conv_w, cb, gw, gb, lam, l,
                           n_seq=DEC_BATCH, seq=DEC_SEQ, tok0=T_CTX, ct=256, filled=rec)
        new_lru.append(lru_f)

        s5, f_r, f_i = _s5_call(us, zeros_s5, zeros_s5, 0, ab_r, ab_i, wb, wcr, wci, d_skip, l,
                                n_blk=BATCH // S5_SEQS, seq=SEQ, row0=0, filled=s5)
        s5, _, _ = _s5_call(us, lat_s5_re, lat_s5_im, l, ab_r, ab_i, wb, wcr, wci, d_skip, l,
                            n_blk=DEC_BATCH // S5_SEQS, seq=DEC_SEQ, row0=S5_LAT_ROW0, filled=s5)
        new_sr.append(f_r)
        new_si.append(f_i)

        x, h2 = _merge_out(attn, rec, s5, gates, glu_w_bf, glu_b, w_pa_bf, w_pr_bf, w_ps_bf, w_o_bf,
                           x, mod, g_ffn, l)

        hmid = _ffn_up(h2, w_up, ffn_conv_w, ffn_conv_b, l)
        if l + 1 < DEPTH:
            x, h = _residual_proj(hmid, w_down_bf, x, mod, l, 5, "ffn_down", tm=512,
                                  norm=(g_mix, l + 1, 0, 1))
        else:
            x, = _residual_proj(hmid, w_down_bf, x, mod, l, 5, "ffn_down", tm=512)

    def final_states(parts):
        st = jnp.stack(parts, axis=0).reshape(DEPTH, BATCH // S5_SEQS, S5_PAIRS, 2, 2, S5_SEQS, S5_STATES)
        st = st.transpose(1, 5, 0, 3, 2, 4, 6)
        return st.reshape(BATCH, DEPTH, 2, N_SSM_GROUPS, SSM_STATE)

    cache_shape = (BATCH, DEPTH, SEQ, N_KV_HEADS, HEAD_DIM)
    return (x[:T_CTX].reshape(BATCH, SEQ, D_MODEL), x[T_CTX:].reshape(DEC_BATCH, DEC_SEQ, D_MODEL),
            new_k.reshape(cache_shape), new_v.reshape(cache_shape),
            jnp.stack(new_lru, axis=0).transpose(1, 3, 0, 2, 4).reshape(BATCH, DEPTH, 2, D_RNN),
            final_states(new_sr), final_states(new_si))
```

```python
import functools
import math

import jax
import jax.numpy as jnp
from jax import lax
from jax.experimental import pallas as pl
from jax.experimental.pallas import tpu as pltpu

F32 = jnp.float32
BF = jnp.bfloat16

D_MODEL = 2048
BATCH = 32
SEQ = 256
DEPTH = 4
DEC_BATCH = 4
DEC_SEQ = 1024
PAST_LEN = 256
GRID_W = 64
N_HEADS = 8
N_KV_HEADS = 2
HEAD_DIM = 128
AXIS_DIM = HEAD_DIM // 2
ROPE_THETA = 10000.0
D_ATTN = N_HEADS * HEAD_DIM
D_KV = N_KV_HEADS * HEAD_DIM
D_RNN = D_MODEL // 2
RNN_BLOCK = 128
N_RNN_BLOCKS = D_RNN // RNN_BLOCK
RNN_CONV_W = 4
LRU_C = 8.0
D_SSM = D_MODEL // 2
SSM_GROUP = 16
N_SSM_GROUPS = D_SSM // SSM_GROUP
SSM_STATE = 64
D_FF = 2 * D_MODEL
N_MOD = 6
EPS = 1e-6
D_IN = D_ATTN + 2 * D_KV + 2 * D_RNN + D_SSM + 3 * D_MODEL

OFF_Q = 0
OFF_K = D_ATTN
OFF_V = OFF_K + D_KV
OFF_XR = OFF_V + D_KV
OFF_XG = OFF_XR + D_RNN
OFF_US = OFF_XG + D_RNN
OFF_GL = OFF_US + D_SSM

T_CTX = BATCH * SEQ
T_LAT = DEC_BATCH * DEC_SEQ
T_ALL = T_CTX + T_LAT
MOD_ROWS = 8

SUBLANES = 8
LANES = 128
VMEM_LIMIT = 56 * 1024 * 1024

S5_SEQS = SUBLANES // 2
S5_GROUPS = LANES // SSM_GROUP
S5_TILES = N_SSM_GROUPS // S5_GROUPS
S5_PAIRS = S5_TILES // 2
S5_STATES = S5_GROUPS * SSM_STATE
S5_CHUNK = 256
S5_GROUP = 32
S5_LAT_ROW0 = (BATCH // S5_SEQS) * SEQ
S5_ROWS = S5_LAT_ROW0 + DEC_SEQ


def _params(n_axes):
    return pltpu.CompilerParams(dimension_semantics=("arbitrary",) * n_axes,
                                vmem_limit_bytes=VMEM_LIMIT)


def _mod_row(first_token):
    return jnp.where(first_token < T_CTX, 0, 1 + jnp.maximum(first_token - T_CTX, 0) // DEC_SEQ)


def _gelu(x):
    return jax.nn.gelu(x)


def _sigmoid_tanh(x):
    return 0.5 * jnp.tanh(0.5 * x) + 0.5


def _mod_kernel(c_ref, w_ref, b_ref, o_ref):
    c = c_ref[...]
    s = (c * jax.nn.sigmoid(c)).astype(BF)
    o_ref[...] = jnp.dot(s, w_ref[...].astype(BF), preferred_element_type=F32) + b_ref[...]


def _modulation(cvec, w_mod, b_mod):
    tn = 1536
    n_out = N_MOD * D_MODEL
    return pl.pallas_call(
        _mod_kernel,
        grid=(DEPTH, n_out // tn),
        in_specs=[pl.BlockSpec((MOD_ROWS, D_MODEL), lambda l, j: (0, 0)),
                  pl.BlockSpec((None, D_MODEL, tn), lambda l, j: (l, 0, j)),
                  pl.BlockSpec((None, 1, tn), lambda l, j: (l, 0, j))],
        out_specs=pl.BlockSpec((None, MOD_ROWS, tn), lambda l, j: (l, 0, j)),
        out_shape=jax.ShapeDtypeStruct((DEPTH, MOD_ROWS, n_out), F32),
        compiler_params=_params(2),
        name="modulation",
    )(cvec, w_mod, b_mod.reshape(DEPTH, 1, n_out))


def _norm_kernel(x_ref, g_ref, sc_ref, sh_ref, o_ref, *, tm):
    r = _mod_row(pl.program_id(0) * tm)
    x = x_ref[...]
    y = x * lax.rsqrt(jnp.mean(x * x, axis=-1, keepdims=True) + EPS)
    y = y * g_ref[...]
    sc = sc_ref[pl.ds(r, 1), :]
    sh = sh_ref[pl.ds(r, 1), :]
    o_ref[...] = (y * (1.0 + sc) + sh).astype(BF)


def _norm_mod(x, gain, mod, layer, shift_idx, scale_idx):
    tm = 512
    return pl.pallas_call(
        functools.partial(_norm_kernel, tm=tm),
        grid=(T_ALL // tm,),
        in_specs=[pl.BlockSpec((tm, D_MODEL), lambda i: (i, 0)),
                  pl.BlockSpec((None, 1, D_MODEL), lambda i: (layer, 0, 0)),
                  pl.BlockSpec((None, MOD_ROWS, D_MODEL), lambda i: (layer, 0, scale_idx)),
                  pl.BlockSpec((None, MOD_ROWS, D_MODEL), lambda i: (layer, 0, shift_idx))],
        out_specs=pl.BlockSpec((tm, D_MODEL), lambda i: (i, 0)),
        out_shape=jax.ShapeDtypeStruct((T_ALL, D_MODEL), BF),
        compiler_params=_params(1),
        name="norm_mod",
    )(x, gain.reshape(DEPTH, 1, D_MODEL), mod, mod)


def _ws_kernel(*refs, n_a, n_w, n_extra, n_out, lhs_of, epilogue, tm):
    a_refs = refs[:n_a]
    w_refs = refs[n_a:n_a + n_w]
    extra = refs[n_a + n_w:n_a + n_w + n_extra]
    outs = refs[n_a + n_w + n_extra:n_a + n_w + n_extra + n_out]
    wbf = refs[n_a + n_w + n_extra + n_out:]
    j = pl.program_id(0)
    i = pl.program_id(1)

    @pl.when(i == 0)
    def _():
        for k in range(n_w):
            wbf[k][...] = w_refs[k][...].astype(BF)

    accs = [jnp.dot(a_refs[lhs_of[k]][...], wbf[k][...], preferred_element_type=F32)
            for k in range(n_w)]
    epilogue(accs, extra, outs, j, i * tm)


def _ws_matmul(a_list, w_list, extra, out_shapes, out_specs, epilogue, *, lhs_of, tm, tn, n_tiles, name):
    in_specs, args, scratch = [], [], []
    w_mode = {}
    for a in a_list:
        in_specs.append(pl.BlockSpec((tm, a.shape[1]), lambda j, i: (i, 0)))
        args.append(a)
    for w, layer, off in w_list:
        kdim = w.shape[1]
        in_specs.append(pl.BlockSpec((None, kdim, tn), lambda j, i, layer=layer, off=off: (layer, 0, off + j),
                                     **w_mode))
        args.append(w)
        scratch.append(pltpu.VMEM((kdim, tn), BF))
    for arr, spec in extra:
        in_specs.append(spec)
        args.append(arr)
    kern = functools.partial(_ws_kernel, n_a=len(a_list), n_w=len(w_list), n_extra=len(extra),
                             n_out=len(out_shapes), lhs_of=lhs_of, epilogue=epilogue, tm=tm)
    return pl.pallas_call(
        kern,
        grid=(n_tiles, T_ALL // tm),
        in_specs=in_specs,
        out_specs=out_specs,
        out_shape=out_shapes,
        scratch_shapes=scratch,
        compiler_params=_params(2),
        name=name,
    )(*args)


def _segment_block(i):
    il = i - BATCH
    per_lat = DEC_SEQ // SEQ
    slot = jnp.where(i < BATCH, i % S5_SEQS, il // per_lat)
    rblk = jnp.where(i < BATCH, i // S5_SEQS, S5_LAT_ROW0 // SEQ + il % per_lat)
    return slot, rblk


def _store_acc(accs, extra, outs, j, row0):
    outs[0][...] = accs[0]


def _in_proj(h, w_in, layer):
    tm = 1024
    tn_a = 896
    z_a = _ws_matmul([h], [(w_in, layer, 0)], [],
                     [jax.ShapeDtypeStruct((T_ALL, OFF_US), F32)],
                     [pl.BlockSpec((tm, tn_a), lambda j, i: (i, j))],
                     _store_acc, lhs_of=(0,), tm=tm, tn=tn_a, n_tiles=OFF_US // tn_a, name="in_proj_a")[0]
    tn_g = 1536

    def gate_epilogue(accs, extra, outs, j, row0):
        outs[0][...] = jax.nn.sigmoid(accs[0]).astype(BF)

    z_g = _ws_matmul([h], [(w_in, layer, OFF_GL // tn_g)], [],
                     [jax.ShapeDtypeStruct((T_ALL, 3 * D_MODEL), BF)],
                     [pl.BlockSpec((tm, tn_g), lambda j, i: (i, j))],
                     gate_epilogue, lhs_of=(0,), tm=tm, tn=tn_g, n_tiles=3 * D_MODEL // tn_g,
                     name="in_proj_g")[0]

    tn_u = 512

    def us_index(j, i):
        slot, rblk = _segment_block(i)
        return slot, rblk, 0

    def us_epilogue(accs, extra, outs, j, row0):
        outs[0][...] = jnp.concatenate(accs, axis=1)

    us = _ws_matmul([h], [(w_in, layer, OFF_US // tn_u), (w_in, layer, OFF_US // tn_u + 1)], [],
                    [jax.ShapeDtypeStruct((S5_SEQS, S5_ROWS, D_SSM), F32)],
                    [pl.BlockSpec((None, SEQ, D_SSM), us_index)],
                    us_epilogue, lhs_of=(0, 0), tm=SEQ, tn=tn_u, n_tiles=1, name="in_proj_u")[0]
    return z_a, z_g, us


def _merge_out_kernel(attn_ref, rec_ref, s5_ref, g_ref, wglu_ref, bglu_ref, wpa_ref, wpr_ref, wps_ref, wo_ref,
                      x_ref, gate_ref, gain_ref, sc_ref, sh_ref, x_out, h_out, *, tm):
    r = _mod_row(pl.program_id(0) * tm)
    s5 = s5_ref[...]
    glu = jnp.dot(s5.astype(BF), wglu_ref[...], preferred_element_type=F32) + bglu_ref[...]
    ssm = (s5 * jax.nn.sigmoid(glu)).astype(BF)
    m = g_ref[:, 0:D_MODEL].astype(F32) * jnp.dot(attn_ref[...], wpa_ref[...], preferred_element_type=F32)
    m = m + (g_ref[:, D_MODEL:2 * D_MODEL].astype(F32)
             * jnp.dot(rec_ref[...], wpr_ref[...], preferred_element_type=F32))
    m = m + (g_ref[:, 2 * D_MODEL:3 * D_MODEL].astype(F32)
             * jnp.dot(ssm, wps_ref[...], preferred_element_type=F32))
    acc = jnp.dot(m.astype(BF), wo_ref[...], preferred_element_type=F32)
    x = x_ref[...] + gate_ref[pl.ds(r, 1), :] * acc
    x_out[...] = x
    y = x * lax.rsqrt(jnp.mean(x * x, axis=-1, keepdims=True) + EPS)
    y = y * gain_ref[...]
    h_out[...] = (y * (1.0 + sc_ref[pl.ds(r, 1), :]) + sh_ref[pl.ds(r, 1), :]).astype(BF)


def _merge_out(attn, rec, s5, gates, glu_w, glu_b, w_pa, w_pr, w_ps, w_o, x, mod, g_ffn, layer):
    tm = SEQ
    row = pl.BlockSpec((tm, D_MODEL), lambda i: (i, 0))
    branch = pl.BlockSpec((tm, D_ATTN), lambda i: (i, 0))
    resident = dict(pipeline_mode=pl.Buffered(1))
    wbr = pl.BlockSpec((None, D_ATTN, D_MODEL), lambda i: (layer, 0, 0), **resident)

    def mod_spec(idx):
        return pl.BlockSpec((None, MOD_ROWS, D_MODEL), lambda i: (layer, 0, idx))

    def seg_index(i):
        slot, rblk = _segment_block(i)
        return slot, rblk, 0

    return pl.pallas_call(
        functools.partial(_merge_out_kernel, tm=tm),
        grid=(T_ALL // tm,),
        in_specs=[branch, branch,
                  pl.BlockSpec((None, SEQ, D_SSM), seg_index),
                  pl.BlockSpec((tm, 3 * D_MODEL), lambda i: (i, 0)),
                  pl.BlockSpec((None, D_SSM, D_SSM), lambda i: (layer, 0, 0), **resident),
                  pl.BlockSpec((None, 1, D_SSM), lambda i: (layer, 0, 0)),
                  wbr, wbr, wbr,
                  pl.BlockSpec((None, D_MODEL, D_MODEL), lambda i: (layer, 0, 0), **resident),
                  row, mod_spec(2),
                  pl.BlockSpec((None, 1, D_MODEL), lambda i: (layer, 0, 0)), mod_spec(4), mod_spec(3)],
        out_specs=[row, row],
        out_shape=[jax.ShapeDtypeStruct((T_ALL, D_MODEL), F32), jax.ShapeDtypeStruct((T_ALL, D_MODEL), BF)],
        compiler_params=_params(1),
        name="merge_out",
    )(attn, rec, s5, gates, glu_w, glu_b.reshape(DEPTH, 1, D_SSM), w_pa, w_pr, w_ps, w_o, x, mod,
      g_ffn.reshape(DEPTH, 1, D_MODEL), mod, mod)


def _residual_kernel(a_ref, w_ref, x_ref, gate_ref, *rest, tm, with_norm):
    r = _mod_row(pl.program_id(0) * tm)
    acc = jnp.dot(a_ref[...], w_ref[...], preferred_element_type=F32)
    x = x_ref[...] + gate_ref[pl.ds(r, 1), :] * acc
    if not with_norm:
        rest[0][...] = x
        return
    g_ref, sc_ref, sh_ref, x_out, h_out = rest
    x_out[...] = x
    y = x * lax.rsqrt(jnp.mean(x * x, axis=-1, keepdims=True) + EPS)
    y = y * g_ref[...]
    h_out[...] = (y * (1.0 + sc_ref[pl.ds(r, 1), :]) + sh_ref[pl.ds(r, 1), :]).astype(BF)


def _residual_proj(a, w_bf, x, mod, layer, gate_idx, name, *, tm, norm=None):
    kdim = a.shape[1]
    row = pl.BlockSpec((tm, D_MODEL), lambda i: (i, 0))
    in_specs = [pl.BlockSpec((tm, kdim), lambda i: (i, 0)),
                pl.BlockSpec((None, kdim, D_MODEL), lambda i: (layer, 0, 0), pipeline_mode=pl.Buffered(1)),
                row,
                pl.BlockSpec((None, MOD_ROWS, D_MODEL), lambda i: (layer, 0, gate_idx))]
    args = [a, w_bf, x, mod]
    out_shape = [jax.ShapeDtypeStruct((T_ALL, D_MODEL), F32)]
    out_specs = [row]
    if norm is not None:
        gain, nl, shift_idx, scale_idx = norm
        in_specs += [pl.BlockSpec((None, 1, D_MODEL), lambda i: (nl, 0, 0)),
                     pl.BlockSpec((None, MOD_ROWS, D_MODEL), lambda i: (nl, 0, scale_idx)),
                     pl.BlockSpec((None, MOD_ROWS, D_MODEL), lambda i: (nl, 0, shift_idx))]
        args += [gain.reshape(DEPTH, 1, D_MODEL), mod, mod]
        out_shape.append(jax.ShapeDtypeStruct((T_ALL, D_MODEL), BF))
        out_specs.append(row)
    return pl.pallas_call(
        functools.partial(_residual_kernel, tm=tm, with_norm=norm is not None),
        grid=(T_ALL // tm,),
        in_specs=in_specs,
        out_specs=out_specs,
        out_shape=out_shape,
        compiler_params=_params(1),
        name=name,
    )(*args)


def _ffn_up(h2, w_up, conv_w, conv_b, layer):
    tm, tn = DEC_SEQ, 512

    def epilogue(accs, extra, outs, j, row0):
        a, b = accs
        seq_len = jnp.where(row0 < T_CTX, SEQ, DEC_SEQ)
        pos = lax.broadcasted_iota(jnp.int32, a.shape, 0) & (seq_len - 1)
        prev = jnp.where(pos == 0, 0.0, pltpu.roll(a, 1, 0))
        nxt = jnp.where(pos == seq_len - 1, 0.0, pltpu.roll(a, tm - 1, 0))
        cw = extra[0][...]
        a = cw[0:1] * prev + cw[1:2] * a + cw[2:3] * nxt + extra[1][...]
        outs[0][...] = (_gelu(a) * b).astype(BF)

    extra = [(conv_w, pl.BlockSpec((None, 3, tn), lambda j, i: (layer, 0, j))),
             (conv_b.reshape(DEPTH, 1, D_FF), pl.BlockSpec((None, 1, tn), lambda j, i: (layer, 0, j)))]
    return _ws_matmul([h2], [(w_up, layer, 0), (w_up, layer, D_FF // tn)], extra,
                      [jax.ShapeDtypeStruct((T_ALL, D_FF), BF)],
                      [pl.BlockSpec((tm, tn), lambda j, i: (i, j))],
                      epilogue, lhs_of=(0, 0), tm=tm, tn=tn, n_tiles=D_FF // tn, name="ffn_up")[0]


def _head_norm(x, gain):
    return x * lax.rsqrt(jnp.mean(x * x, axis=-1, keepdims=True) + EPS) * gain


def _rope(x, cos, sin_signed):
    lane = lax.broadcasted_iota(jnp.int32, x.shape, 1)
    first_half = (lane & (AXIS_DIM - 1)) < AXIS_DIM // 2
    partner = jnp.where(first_half, pltpu.roll(x, HEAD_DIM - AXIS_DIM // 2, 1),
                        pltpu.roll(x, AXIS_DIM // 2, 1))
    return x * cos + partner * sin_signed


def _scores(q, k):
    return lax.dot_general(q, k, (((1,), (1,)), ((), ())), preferred_element_type=F32) * (HEAD_DIM ** -0.5)


def _attn_ctx_kernel(q_ref, k_ref, v_ref, qg_ref, kg_ref, keep_a, keep_k, keep_v, o_ref, ko_ref, vo_ref):
    del keep_a, keep_k, keep_v
    grp = N_HEADS // N_KV_HEADS
    vo_ref[...] = v_ref[...]
    kn, vv = [], []
    for kv in range(N_KV_HEADS):
        ks = slice(kv * HEAD_DIM, (kv + 1) * HEAD_DIM)
        k = _head_norm(k_ref[:, ks], kg_ref[...])
        ko_ref[:, ks] = k
        kn.append(k.astype(BF))
        vv.append(v_ref[:, ks].astype(BF))

    def head_scores(h):
        qn = _head_norm(q_ref[:, h * HEAD_DIM:(h + 1) * HEAD_DIM], qg_ref[...]).astype(BF)
        return _scores(qn, kn[h // grp])

    s_next = head_scores(0)
    for h in range(N_HEADS):
        s = s_next
        if h + 1 < N_HEADS:
            s_next = head_scores(h + 1)
        p = jnp.exp(s - jnp.max(s, axis=-1, keepdims=True))
        den = jnp.sum(p, axis=-1, keepdims=True)
        o = jnp.dot(p.astype(BF), vv[h // grp], preferred_element_type=F32) / den
        o_ref[:, h * HEAD_DIM:(h + 1) * HEAD_DIM] = o.astype(BF)


def _attn_lat_kernel(q_ref, k_ref, v_ref, ck_ref, cv_ref, cosq_ref, sinq_ref, cosk_ref, sink_ref,
                     qg_ref, kg_ref, filled_ref, o_ref):
    del filled_ref
    grp = N_HEADS // N_KV_HEADS
    kn, vv, ck, cv = [], [], [], []
    for kv in range(N_KV_HEADS):
        ks = slice(kv * HEAD_DIM, (kv + 1) * HEAD_DIM)
        kn.append(_rope(_head_norm(k_ref[:, ks], kg_ref[...]), cosk_ref[...], sink_ref[...]).astype(BF))
        vv.append(v_ref[:, ks].astype(BF))
        ck.append(ck_ref[:, ks].astype(BF))
        cv.append(cv_ref[:, ks].astype(BF))

    def head_scores(h):
        q = _head_norm(q_ref[:, h * HEAD_DIM:(h + 1) * HEAD_DIM], qg_ref[...])
        qn = _rope(q, cosq_ref[...], sinq_ref[...]).astype(BF)
        return _scores(qn, ck[h // grp]), _scores(qn, kn[h // grp])

    s_next = head_scores(0)
    for h in range(N_HEADS):
        s1, s2 = s_next
        if h + 1 < N_HEADS:
            s_next = head_scores(h + 1)
        m = jnp.maximum(jnp.max(s1, axis=-1, keepdims=True), jnp.max(s2, axis=-1, keepdims=True))
        p1 = jnp.exp(s1 - m)
        p2 = jnp.exp(s2 - m)
        den = jnp.sum(p1, axis=-1, keepdims=True) + jnp.sum(p2, axis=-1, keepdims=True)
        o = (jnp.dot(p1.astype(BF), cv[h // grp], preferred_element_type=F32)
             + jnp.dot(p2.astype(BF), vv[h // grp], preferred_element_type=F32)) / den
        o_ref[:, h * HEAD_DIM:(h + 1) * HEAD_DIM] = o.astype(BF)


def _attention(z, q_gain, k_gain, cache_k, cache_v, cos_t, sin_t, layer, attn_buf, new_k, new_v):
    gains = [pl.BlockSpec((None, 1, HEAD_DIM), lambda *_: (layer, 0, 0))] * 2
    qg = q_gain.reshape(DEPTH, 1, HEAD_DIM)
    kg = k_gain.reshape(DEPTH, 1, HEAD_DIM)
    kcol, vcol = OFF_K // D_KV, OFF_V // D_KV
    keep = pl.BlockSpec(memory_space=pl.ANY)
    cache_spec = pl.BlockSpec((None, None, SEQ, D_KV), lambda b: (b, layer, 0, 0))
    attn, new_k, new_v = pl.pallas_call(
        _attn_ctx_kernel,
        grid=(BATCH,),
        in_specs=[pl.BlockSpec((SEQ, D_ATTN), lambda b: (b, 0)),
                  pl.BlockSpec((SEQ, D_KV), lambda b: (b, kcol)),
                  pl.BlockSpec((SEQ, D_KV), lambda b: (b, vcol))] + gains + [keep, keep, keep],
        out_specs=[pl.BlockSpec((SEQ, D_ATTN), lambda b: (b, 0)), cache_spec, cache_spec],
        out_shape=[jax.ShapeDtypeStruct((T_ALL, D_ATTN), BF),
                   jax.ShapeDtypeStruct((BATCH, DEPTH, SEQ, D_KV), F32),
                   jax.ShapeDtypeStruct((BATCH, DEPTH, SEQ, D_KV), F32)],
        input_output_aliases={5: 0, 6: 1, 7: 2},
        compiler_params=_params(1),
        name="attn_ctx",
    )(z, z, z, qg, kg, attn_buf, new_k, new_v)

    tq = 512
    nq = DEC_SEQ // tq
    lat0 = T_CTX // DEC_SEQ
    attn = pl.pallas_call(
        _attn_lat_kernel,
        grid=(DEC_BATCH, nq),
        in_specs=[pl.BlockSpec((tq, D_ATTN), lambda b, i: ((lat0 + b) * nq + i, 0)),
                  pl.BlockSpec((DEC_SEQ, D_KV), lambda b, i: (lat0 + b, kcol)),
                  pl.BlockSpec((DEC_SEQ, D_KV), lambda b, i: (lat0 + b, vcol)),
                  pl.BlockSpec((None, None, PAST_LEN, D_KV), lambda b, i: (b, layer, 0, 0)),
                  pl.BlockSpec((None, None, PAST_LEN, D_KV), lambda b, i: (b, layer, 0, 0)),
                  pl.BlockSpec((tq, HEAD_DIM), lambda b, i: (i, 0)),
                  pl.BlockSpec((tq, HEAD_DIM), lambda b, i: (i, 0)),
                  pl.BlockSpec((DEC_SEQ, HEAD_DIM), lambda b, i: (0, 0)),
                  pl.BlockSpec((DEC_SEQ, HEAD_DIM), lambda b, i: (0, 0))] + gains
                 + [pl.BlockSpec(memory_space=pl.ANY)],
        out_specs=pl.BlockSpec((tq, D_ATTN), lambda b, i: ((lat0 + b) * nq + i, 0)),
        out_shape=jax.ShapeDtypeStruct((T_ALL, D_ATTN), BF),
        input_output_aliases={11: 0},
        compiler_params=_params(2),
        name="attn_lat",
    )(z, z, z, cache_k, cache_v, cos_t, sin_t, cos_t, sin_t, qg, kg, attn)
    return attn, new_k, new_v


def _lru_kernel(xr_ref, xg_ref, h0_ref, cw_ref, cb_ref, gw_ref, gb_ref, lam_ref, *rest, seq, ct):
    o_ref, hf_ref, a_s, b_s, hfw_s, hbw_s = rest[-6:]
    n_blocks = ct // RNN_BLOCK
    n_tiles = seq // SUBLANES
    row = lax.broadcasted_iota(jnp.int32, (seq, ct), 0)
    xr = xr_ref[...]

    def shifted(x, k):
        if k > 0:
            return jnp.where(row >= k, pltpu.roll(x, k, 0), 0.0)
        return jnp.where(row < seq + k, pltpu.roll(x, seq + k, 0), 0.0)

    cw = cw_ref[...]
    xc = (cb_ref[...] + cw[0:1] * shifted(xr, 2) + cw[1:2] * shifted(xr, 1) + cw[2:3] * xr
          + cw[3:4] * shifted(xr, -1))

    for kb in range(n_blocks):
        cs = slice(kb * RNN_BLOCK, (kb + 1) * RNN_BLOCK)
        xcb = xc[:, cs]
        gates = jnp.dot(xcb.astype(BF), gw_ref[kb], preferred_element_type=F32) + gb_ref[kb]
        for d in range(2):
            a, b = _lru_gate_terms(xcb, gates, lam_ref, kb, d)
            a_s[d, :, cs] = a
            b_s[d, :, cs] = b

    sub = lax.broadcasted_iota(jnp.int32, (SUBLANES, ct), 0)

    def tile_scan(a, b, forward):
        for k in (1, 2, 4):
            if forward:
                keep = sub >= k
                shift = k
            else:
                keep = sub < SUBLANES - k
                shift = SUBLANES - k
            a_n = jnp.where(keep, pltpu.roll(a, shift, 0), 1.0)
            b_n = jnp.where(keep, pltpu.roll(b, shift, 0), 0.0)
            b = a * b_n + b
            a = a * a_n
        return a, b

    def body(j, carry):
        hf, hb = carry
        r0 = pl.multiple_of(j * SUBLANES, SUBLANES)
        a, b = tile_scan(a_s[0, pl.ds(r0, SUBLANES), :], b_s[0, pl.ds(r0, SUBLANES), :], True)
        h = a * hf + b
        hfw_s[pl.ds(r0, SUBLANES), :] = h
        hf = h[SUBLANES - 1:SUBLANES, :]
        r1 = pl.multiple_of((n_tiles - 1 - j) * SUBLANES, SUBLANES)
        a, b = tile_scan(a_s[1, pl.ds(r1, SUBLANES), :], b_s[1, pl.ds(r1, SUBLANES), :], False)
        h = a * hb + b
        hbw_s[pl.ds(r1, SUBLANES), :] = h
        hb = h[0:1, :]
        return hf, hb

    hf, hb = lax.fori_loop(0, n_tiles, body, (h0_ref[0:1, :], h0_ref[1:2, :]))
    hf_ref[0:1, :] = hf
    hf_ref[1:2, :] = hb
    o_ref[...] = (_gelu(xg_ref[...]) * (hfw_s[...] + hbw_s[...])).astype(BF)


def _lru_call(z, h0, cw, cb, gw, gb, lam, layer, *, n_seq, seq, tok0, ct, filled=None):
    nc = D_RNN // ct
    s0 = tok0 // seq
    blk = ct // RNN_BLOCK
    alias = {} if filled is None else dict(input_output_aliases={8: 0})
    alias_spec = [] if filled is None else [pl.BlockSpec(memory_space=pl.ANY)]
    alias_arg = [] if filled is None else [filled]
    return pl.pallas_call(
        functools.partial(_lru_kernel, seq=seq, ct=ct),
        grid=(n_seq, nc),
        in_specs=[pl.BlockSpec((seq, ct), lambda s, c: (s0 + s, OFF_XR // ct + c)),
                  pl.BlockSpec((seq, ct), lambda s, c: (s0 + s, OFF_XG // ct + c)),
                  pl.BlockSpec((None, None, 2, ct), lambda s, c: (s, layer, 0, c)),
                  pl.BlockSpec((None, RNN_CONV_W, ct), lambda s, c: (layer, 0, c)),
                  pl.BlockSpec((None, 1, ct), lambda s, c: (layer, 0, c)),
                  pl.BlockSpec((None, blk, RNN_BLOCK, 4 * RNN_BLOCK), lambda s, c: (layer, c, 0, 0)),
                  pl.BlockSpec((None, blk, 1, 4 * RNN_BLOCK), lambda s, c: (layer, c, 0, 0)),
                  pl.BlockSpec((None, blk, 2, RNN_BLOCK), lambda s, c: (layer, c, 0, 0))] + alias_spec,
        out_specs=[pl.BlockSpec((seq, ct), lambda s, c: (s0 + s, c)),
                   pl.BlockSpec((None, 2, ct), lambda s, c: (s, 0, c))],
        out_shape=[jax.ShapeDtypeStruct((T_ALL, D_RNN), BF),
                   jax.ShapeDtypeStruct((n_seq, 2, D_RNN), F32)],
        scratch_shapes=[pltpu.VMEM((2, seq, ct), F32), pltpu.VMEM((2, seq, ct), F32),
                        pltpu.VMEM((seq, ct), F32), pltpu.VMEM((seq, ct), F32)],
        compiler_params=_params(2),
        name="rg_lru",
        **alias,
    )(z, z, h0, cw, cb, gw, gb, lam, *alias_arg)


def _lru_gate_terms(xc, gates, lam_ref, kb, d):
    base = d * 2 * RNN_BLOCK
    r = _sigmoid_tanh(gates[:, base:base + RNN_BLOCK])
    ig = _sigmoid_tanh(gates[:, base + RNN_BLOCK:base + 2 * RNN_BLOCK])
    lam = lam_ref[kb, d:d + 1, :]
    a = jnp.exp(r * (-LRU_C * jnp.log1p(jnp.exp(-lam))))
    om = 1.0 - a * a
    root = jnp.where(om > 0.0, om * lax.rsqrt(om), 0.0)
    return a, root * (ig * xc)


LRU_CTX_SEQS = SUBLANES
LRU_PAD_FRONT = (RNN_CONV_W // 2) * SUBLANES
LRU_PAD_BACK = (RNN_CONV_W - 1 - RNN_CONV_W // 2) * SUBLANES


def _lru_ctx_kernel(xr_ref, xg_ref, cw_ref, cb_ref, gw_ref, gb_ref, lam_ref, keep_ref, o_ref, hf_ref,
                    xs, a_s, b_s, hfw, hbw, *, ct):
    del keep_ref
    n_blocks = ct // RNN_BLOCK
    rows = SEQ * SUBLANES

    @pl.when((pl.program_id(0) == 0) & (pl.program_id(1) == 0))
    def _():
        xs[...] = jnp.zeros_like(xs)

    for kb in range(n_blocks):
        cs = slice(kb * RNN_BLOCK, (kb + 1) * RNN_BLOCK)
        for b in range(LRU_CTX_SEQS):
            xs[kb, pl.ds(LRU_PAD_FRONT + b, SEQ, stride=SUBLANES), :] = xr_ref[b, :, cs]

    for kb in range(n_blocks):
        cs = slice(kb * RNN_BLOCK, (kb + 1) * RNN_BLOCK)
        xc = cb_ref[:, cs]
        for j in range(RNN_CONV_W):
            xc = xc + cw_ref[j:j + 1, cs] * xs[kb, j * SUBLANES:j * SUBLANES + rows, :]
        gates = jnp.dot(xc.astype(BF), gw_ref[kb], preferred_element_type=F32) + gb_ref[kb]
        for d in range(2):
            a, b = _lru_gate_terms(xc, gates, lam_ref, kb, d)
            a_s[d, kb] = a
            b_s[d, kb] = b

    def body(t, carry):
        hf, hb = carry
        r0 = pl.multiple_of(t * SUBLANES, SUBLANES)
        r1 = pl.multiple_of((SEQ - 1 - t) * SUBLANES, SUBLANES)
        new_f, new_b = [], []
        for kb in range(n_blocks):
            f = a_s[0, kb, pl.ds(r0, SUBLANES), :] * hf[kb] + b_s[0, kb, pl.ds(r0, SUBLANES), :]
            hfw[kb, pl.ds(r0, SUBLANES), :] = f
            new_f.append(f)
            g = a_s[1, kb, pl.ds(r1, SUBLANES), :] * hb[kb] + b_s[1, kb, pl.ds(r1, SUBLANES), :]
            hbw[kb, pl.ds(r1, SUBLANES), :] = g
            new_b.append(g)
        return tuple(new_f), tuple(new_b)

    zero = tuple(jnp.zeros((SUBLANES, RNN_BLOCK), F32) for _ in range(n_blocks))
    hf, hb = lax.fori_loop(0, SEQ, body, (zero, zero), unroll=8)
    for kb in range(n_blocks):
        cs = slice(kb * RNN_BLOCK, (kb + 1) * RNN_BLOCK)
        hf_ref[0, :, cs] = hf[kb]
        hf_ref[1, :, cs] = hb[kb]
        for b in range(LRU_CTX_SEQS):
            h = (hfw[kb, pl.ds(b, SEQ, stride=SUBLANES), :] + hbw[kb, pl.ds(b, SEQ, stride=SUBLANES), :])
            o_ref[b, :, cs] = (_gelu(xg_ref[b, :, cs]) * h).astype(BF)


def _lru_ctx_call(z, cw, cb, gw, gb, lam, layer, filled):
    ct = 256
    blk = ct // RNN_BLOCK
    n_seg = T_ALL // SEQ
    z3 = z.reshape(n_seg, SEQ, z.shape[1])
    rows = SEQ * SUBLANES
    pad_rows = LRU_PAD_FRONT + rows + LRU_PAD_BACK
    tok = pl.BlockSpec((LRU_CTX_SEQS, SEQ, ct), lambda s, c: (s, 0, c))
    rec, fin = pl.pallas_call(
        functools.partial(_lru_ctx_kernel, ct=ct),
        grid=(BATCH // LRU_CTX_SEQS, D_RNN // ct),
        in_specs=[pl.BlockSpec((LRU_CTX_SEQS, SEQ, ct), lambda s, c: (s, 0, OFF_XR // ct + c)),
                  pl.BlockSpec((LRU_CTX_SEQS, SEQ, ct), lambda s, c: (s, 0, OFF_XG // ct + c)),
                  pl.BlockSpec((None, RNN_CONV_W, ct), lambda s, c: (layer, 0, c)),
                  pl.BlockSpec((None, 1, ct), lambda s, c: (layer, 0, c)),
                  pl.BlockSpec((None, blk, RNN_BLOCK, 4 * RNN_BLOCK), lambda s, c: (layer, c, 0, 0)),
                  pl.BlockSpec((None, blk, 1, 4 * RNN_BLOCK), lambda s, c: (layer, c, 0, 0)),
                  pl.BlockSpec((None, blk, 2, RNN_BLOCK), lambda s, c: (layer, c, 0, 0)),
                  pl.BlockSpec(memory_space=pl.ANY)],
        out_specs=[tok, pl.BlockSpec((None, 2, LRU_CTX_SEQS, ct), lambda s, c: (s, 0, 0, c))],
        out_shape=[jax.ShapeDtypeStruct((n_seg, SEQ, D_RNN), BF),
                   jax.ShapeDtypeStruct((BATCH // LRU_CTX_SEQS, 2, LRU_CTX_SEQS, D_RNN), F32)],
        scratch_shapes=[pltpu.VMEM((blk, pad_rows, RNN_BLOCK), F32),
                        pltpu.VMEM((2, blk, rows, RNN_BLOCK), F32), pltpu.VMEM((2, blk, rows, RNN_BLOCK), F32),
                        pltpu.VMEM((blk, rows, RNN_BLOCK), F32), pltpu.VMEM((blk, rows, RNN_BLOCK), F32)],
        input_output_aliases={7: 0},
        compiler_params=_params(2),
        name="rg_lru_ctx",
    )(z3, z3, cw, cb, gw, gb, lam, filled.reshape(n_seg, SEQ, D_RNN))
    return rec.reshape(T_ALL, D_RNN), fin


def _s5_disc_kernel(are_ref, aim_ref, ldt_ref, bre_ref, bim_ref, abr_ref, abi_ref, bbr_ref, bbi_ref):
    lam_r = jnp.minimum(are_ref[...], -1e-4)
    lam_i = aim_ref[...]
    dt = jnp.exp(ldt_ref[...])
    mag = jnp.exp(lam_r * dt)
    ab_r = mag * jnp.cos(lam_i * dt)
    ab_i = mag * jnp.sin(lam_i * dt)
    den = lam_r * lam_r + lam_i * lam_i
    q_r = ((ab_r - 1.0) * lam_r + ab_i * lam_i) / den
    q_i = (ab_i * lam_r - (ab_r - 1.0) * lam_i) / den
    abr_ref[...] = ab_r
    abi_ref[...] = ab_i
    br = bre_ref[...]
    bi = bim_ref[...]
    bbr_ref[...] = q_r[:, None, :] * br - q_i[:, None, :] * bi
    bbi_ref[...] = q_r[:, None, :] * bi + q_i[:, None, :] * br


def _s5_discretize(a_re, a_im, log_dt, b_re, b_im):
    n = DEPTH * 2 * N_SSM_GROUPS
    bt_re = jnp.swapaxes(b_re, -1, -2).reshape(n, SSM_GROUP, SSM_STATE)
    bt_im = jnp.swapaxes(b_im, -1, -2).reshape(n, SSM_GROUP, SSM_STATE)
    vec = jax.ShapeDtypeStruct((n, SSM_STATE), F32)
    mat = jax.ShapeDtypeStruct((n, SSM_GROUP, SSM_STATE), F32)
    return pl.pallas_call(
        _s5_disc_kernel,
        out_shape=[vec, vec, mat, mat],
        compiler_params=pltpu.CompilerParams(vmem_limit_bytes=VMEM_LIMIT),
        name="s5_discretize",
    )(a_re.reshape(n, SSM_STATE), a_im.reshape(n, SSM_STATE), log_dt.reshape(n, 1), bt_re, bt_im)


def _s5_kernel(u_ref, h0r_ref, h0i_ref, ar_ref, ai_ref, wb_ref, wcr_ref, wci_ref, d_ref, *rest, seq):
    y_ref, fr_ref, fi_ref, lhs_a, lhs_b, acc_s = rest[-6:]
    rows = S5_CHUNK * SUBLANES
    n_chunks = seq // S5_CHUNK

    @pl.when((pl.program_id(0) == 0) & (pl.program_id(1) == 0))
    def _():
        lhs_a[...] = jnp.zeros_like(lhs_a)
        lhs_b[...] = jnp.zeros_like(lhs_b)

    for b in range(S5_SEQS):
        lhs_a[pl.ds(b, seq, stride=SUBLANES), :] = u_ref[b, :, 0:LANES]
        lhs_b[pl.ds(S5_SEQS + b, seq, stride=SUBLANES), :] = u_ref[b, :, LANES:2 * LANES]

    grows = S5_GROUP * SUBLANES
    n_groups = S5_CHUNK // S5_GROUP

    def bu_rows(d, r0, n):
        lhs = jnp.concatenate([lhs_a[pl.ds(r0, n), :], lhs_b[pl.ds(r0, n), :]], axis=1).astype(BF)
        return jnp.dot(lhs, wb_ref[d], preferred_element_type=F32)

    def add_output(d, r0, h_re, h_im):
        y = (jnp.dot(h_re.astype(BF), wcr_ref[d], preferred_element_type=F32)
             - jnp.dot(h_im.astype(BF), wci_ref[d], preferred_element_type=F32))
        rid = lax.broadcasted_iota(jnp.int32, (h_re.shape[0], LANES), 0)
        y = jnp.where((rid & (SUBLANES - 1)) < S5_SEQS, y[:, 0:LANES], y[:, LANES:2 * LANES])
        if d == 0:
            acc_s[pl.ds(r0, h_re.shape[0]), :] = y
        else:
            acc_s[pl.ds(r0, h_re.shape[0]), :] += y

    for d in range(2):
        ar = ar_ref[d]
        ai = ai_ref[d]

        def chunk(c, carry, d=d, ar=ar, ai=ai):
            cc = c if d == 0 else n_chunks - 1 - c
            r0 = pl.multiple_of(cc * rows, rows)
            hr, hi = carry
            order = [g if d == 0 else n_groups - 1 - g for g in range(n_groups)]
            bu_next = bu_rows(d, r0 + order[0] * grows, grows)
            for g in range(n_groups):
                q = r0 + order[g] * grows
                bu = bu_next
                if g + 1 < n_groups:
                    bu_next = bu_rows(d, r0 + order[g + 1] * grows, grows)
                hs_r = [None] * S5_GROUP
                hs_i = [None] * S5_GROUP
                for s in range(S5_GROUP):
                    ss = s if d == 0 else S5_GROUP - 1 - s
                    br = bu[ss * SUBLANES:(ss + 1) * SUBLANES, 0:S5_STATES]
                    bi = bu[ss * SUBLANES:(ss + 1) * SUBLANES, S5_STATES:2 * S5_STATES]
                    hr, hi = ar * hr - ai * hi + br, ar * hi + ai * hr + bi
                    hs_r[ss] = hr
                    hs_i[ss] = hi
                add_output(d, q, jnp.concatenate(hs_r, axis=0), jnp.concatenate(hs_i, axis=0))
            return hr, hi

        hr, hi = lax.fori_loop(0, n_chunks, chunk, (h0r_ref[d], h0i_ref[d]))
        fr_ref[d] = hr
        fi_ref[d] = hi

    for b in range(S5_SEQS):
        y = jnp.concatenate([acc_s[pl.ds(b, seq, stride=SUBLANES), :],
                             acc_s[pl.ds(S5_SEQS + b, seq, stride=SUBLANES), :]], axis=1)
        y_ref[b] = _gelu(y + d_ref[...] * u_ref[b])


def _s5_call(us, h0r, h0i, h0_idx, ab_r, ab_i, wb, wcr, wci, d_skip, layer, *, n_blk, seq, row0, filled=None):
    blk0 = row0 // seq
    st = (None, None, 2, SUBLANES, S5_STATES)
    tok_spec = pl.BlockSpec((S5_SEQS, seq, 2 * LANES), lambda b, p: (0, blk0 + b, p))
    h0_spec = pl.BlockSpec(st, lambda b, p: (h0_idx, p, 0, 0, 0))
    par_spec = pl.BlockSpec(st, lambda b, p: (layer, p, 0, 0, 0))
    fin_spec = pl.BlockSpec(st, lambda b, p: (b, p, 0, 0, 0))
    fin_shape = jax.ShapeDtypeStruct((n_blk, S5_PAIRS, 2, SUBLANES, S5_STATES), F32)
    alias = {} if filled is None else dict(input_output_aliases={9: 0})
    alias_spec = [] if filled is None else [pl.BlockSpec(memory_space=pl.ANY)]
    alias_arg = [] if filled is None else [filled]
    return pl.pallas_call(
        functools.partial(_s5_kernel, seq=seq),
        grid=(n_blk, S5_PAIRS),
        in_specs=[tok_spec, h0_spec, h0_spec, par_spec, par_spec,
                  pl.BlockSpec((None, None, 2, 2 * LANES, 2 * S5_STATES), lambda b, p: (layer, p, 0, 0, 0)),
                  pl.BlockSpec((None, None, 2, S5_STATES, 2 * LANES), lambda b, p: (layer, p, 0, 0, 0)),
                  pl.BlockSpec((None, None, 2, S5_STATES, 2 * LANES), lambda b, p: (layer, p, 0, 0, 0)),
                  pl.BlockSpec((None, 1, 2 * LANES), lambda b, p: (layer, 0, p))] + alias_spec,
        out_specs=[tok_spec, fin_spec, fin_spec],
        out_shape=[jax.ShapeDtypeStruct((S5_SEQS, S5_ROWS, D_SSM), F32), fin_shape, fin_shape],
        scratch_shapes=[pltpu.VMEM((seq * SUBLANES, LANES), F32), pltpu.VMEM((seq * SUBLANES, LANES), F32),
                        pltpu.VMEM((seq * SUBLANES, LANES), F32)],
        compiler_params=_params(2),
        name="s5_scan",
        **alias,
    )(us, h0r, h0i, ab_r, ab_i, wb, wcr, wci, d_skip, *alias_arg)


def _s5_pair_rows(st):
    x = st.reshape(S5_SEQS, DEPTH, 2, S5_PAIRS, 2, S5_STATES)
    return x.transpose(1, 3, 2, 4, 0, 5).reshape(DEPTH, S5_PAIRS, 2, SUBLANES, S5_STATES)


def _rope_tables():
    rows = DEC_SEQ // GRID_W
    t_row = jnp.repeat(jnp.arange(rows, dtype=F32), GRID_W)
    t_col = (jnp.arange(DEC_SEQ) % GRID_W).astype(F32)
    inv = 1.0 / (ROPE_THETA ** (jnp.arange(0, AXIS_DIM, 2, dtype=F32) / AXIS_DIM))
    ang_r = t_row[:, None] * inv
    ang_c = t_col[:, None] * inv
    cos = jnp.concatenate([jnp.cos(ang_r), jnp.cos(ang_r), jnp.cos(ang_c), jnp.cos(ang_c)], axis=-1)
    sin = jnp.concatenate([-jnp.sin(ang_r), jnp.sin(ang_r), -jnp.sin(ang_c), jnp.sin(ang_c)], axis=-1)
    return cos, sin


def _block_diag(blocks):
    n, r, c = blocks.shape[-3:]
    eye = jnp.eye(n, dtype=blocks.dtype)
    out = blocks[..., :, :, None, :] * eye[:, None, :, None]
    return out.reshape(blocks.shape[:-3] + (n * r, n * c))


def kernel(x_prompt, x_sample, c, cache_k, cache_v, state_lru, state_s5_re, state_s5_im, c_ctx, w_mod, b_mod, g_mix, w_in, q_gain, k_gain, rnn_conv_w, rnn_conv_b, lru_gate_w, lru_gate_b, lru_lambda, s5_a_re, s5_a_im, s5_log_dt, s5_b_re, s5_b_im, s5_c_re, s5_c_im, s5_d, glu_w, glu_b, w_pa, w_pr, w_ps, w_o, g_ffn, w_up, ffn_conv_w, ffn_conv_b, w_down):
    x = jnp.concatenate([x_prompt.reshape(T_CTX, D_MODEL), x_sample.reshape(T_LAT, D_MODEL)], axis=0)

    cvec = jnp.zeros((MOD_ROWS, D_MODEL), F32).at[0].set(c_ctx).at[1:1 + DEC_BATCH].set(c)
    mod = _modulation(cvec, w_mod, b_mod)
    cos_t, sin_t = _rope_tables()

    gw = lru_gate_w.transpose(0, 3, 4, 1, 2, 5).reshape(DEPTH, N_RNN_BLOCKS, RNN_BLOCK, 4 * RNN_BLOCK).astype(BF)
    gb = (lru_gate_b.reshape(DEPTH, 2, 2, N_RNN_BLOCKS, RNN_BLOCK).transpose(0, 3, 1, 2, 4)
          .reshape(DEPTH, N_RNN_BLOCKS, 1, 4 * RNN_BLOCK))
    lam = lru_lambda.reshape(DEPTH, 2, N_RNN_BLOCKS, RNN_BLOCK).transpose(0, 2, 1, 3)
    cb = rnn_conv_b.reshape(DEPTH, 1, D_RNN)

    ab_r, ab_i, bbt_r, bbt_i = _s5_discretize(s5_a_re, s5_a_im, s5_log_dt, s5_b_re, s5_b_im)
    def pair_rows(ab):
        x = ab.reshape(DEPTH, 2, S5_PAIRS, 2, 1, S5_STATES)
        x = jnp.broadcast_to(x, (DEPTH, 2, S5_PAIRS, 2, S5_SEQS, S5_STATES))
        return x.transpose(0, 2, 1, 3, 4, 5).reshape(DEPTH, S5_PAIRS, 2, SUBLANES, S5_STATES)

    ab_r = pair_rows(ab_r)
    ab_i = pair_rows(ab_i)
    gshape = (DEPTH, 2, S5_TILES, S5_GROUPS, SSM_GROUP, SSM_STATE)
    wb = jnp.concatenate([_block_diag(bbt_r.reshape(gshape)), _block_diag(bbt_i.reshape(gshape))], axis=-1)
    wb = wb.reshape(DEPTH, 2, S5_PAIRS, 2 * LANES, 2 * S5_STATES)
    wb = wb.transpose(0, 2, 1, 3, 4).astype(BF)

    def pair_cols(c):
        x = _block_diag(jnp.swapaxes(c.reshape(gshape), -1, -2))
        x = x.reshape(DEPTH, 2, S5_PAIRS, 2, S5_STATES, LANES)
        return x.transpose(0, 2, 1, 4, 3, 5).reshape(DEPTH, S5_PAIRS, 2, S5_STATES, 2 * LANES).astype(BF)

    wcr = pair_cols(s5_c_re)
    wci = pair_cols(s5_c_im)
    d_skip = s5_d.reshape(DEPTH, 1, D_SSM)

    zeros_s5 = jnp.zeros((1, S5_PAIRS, 2, SUBLANES, S5_STATES), F32)
    cache_k = cache_k.reshape(DEC_BATCH, DEPTH, PAST_LEN, D_KV)
    cache_v = cache_v.reshape(DEC_BATCH, DEPTH, PAST_LEN, D_KV)

    w_pa_bf, w_pr_bf, w_ps_bf = w_pa.astype(BF), w_pr.astype(BF), w_ps.astype(BF)
    w_o_bf = w_o.astype(BF)
    w_down_bf = w_down.astype(BF)
    glu_w_bf = glu_w.astype(BF)
    lat_s5_re = _s5_pair_rows(state_s5_re)
    lat_s5_im = _s5_pair_rows(state_s5_im)

    attn = jnp.zeros((T_ALL, D_ATTN), BF)
    rec = jnp.zeros((T_ALL, D_RNN), BF)
    s5 = jnp.zeros((S5_SEQS, S5_ROWS, D_SSM), F32)
    new_k = jnp.zeros((BATCH, DEPTH, SEQ, D_KV), F32)
    new_v = jnp.zeros((BATCH, DEPTH, SEQ, D_KV), F32)

    new_lru, new_sr, new_si = [], [], []
    h = _norm_mod(x, g_mix, mod, 0, 0, 1)
    for l in range(DEPTH):
        z, gates, us = _in_proj(h, w_in, l)

        attn, new_k, new_v = _attention(z, q_gain, k_gain, cache_k, cache_v, cos_t, sin_t, l,
                                        attn, new_k, new_v)

        rec, lru_f = _lru_ctx_call(z, rnn_conv_w, cb, gw, gb, lam, l, rec)
        rec, _ = _lru_call(z, state_lru, rnn_conv_w, cb, gw, gb, lam, l,
                           n_seq=DEC_BATCH, seq=DEC_SEQ, tok0=T_CTX, ct=256, filled=rec)
        new_lru.append(lru_f)

        s5, f_r, f_i = _s5_call(us, zeros_s5, zeros_s5, 0, ab_r, ab_i, wb, wcr, wci, d_skip, l,
                                n_blk=BATCH // S5_SEQS, seq=SEQ, row0=0, filled=s5)
        s5, _, _ = _s5_call(us, lat_s5_re, lat_s5_im, l, ab_r, ab_i, wb, wcr, wci, d_skip, l,
                            n_blk=DEC_BATCH // S5_SEQS, seq=DEC_SEQ, row0=S5_LAT_ROW0, filled=s5)
        new_sr.append(f_r)
        new_si.append(f_i)

        x, h2 = _merge_out(attn, rec, s5, gates, glu_w_bf, glu_b, w_pa_bf, w_pr_bf, w_ps_bf, w_o_bf,
                           x, mod, g_ffn, l)

        hmid = _ffn_up(h2, w_up, ffn_conv_w, ffn_conv_b, l)
        if l + 1 < DEPTH:
            x, h = _residual_proj(hmid, w_down_bf, x, mod, l, 5, "ffn_down", tm=512,
                                  norm=(g_mix, l + 1, 0, 1))
        else:
            x, = _residual_proj(hmid, w_down_bf, x, mod, l, 5, "ffn_down", tm=512)

    def final_states(parts):
        st = jnp.stack(parts, axis=0).reshape(DEPTH, BATCH // S5_SEQS, S5_PAIRS, 2, 2, S5_SEQS, S5_STATES)
        st = st.transpose(1, 5, 0, 3, 2, 4, 6)
        return st.reshape(BATCH, DEPTH, 2, N_SSM_GROUPS, SSM_STATE)

    cache_shape = (BATCH, DEPTH, SEQ, N_KV_HEADS, HEAD_DIM)
    return (x[:T_CTX].reshape(BATCH, SEQ, D_MODEL), x[T_CTX:].reshape(DEC_BATCH, DEC_SEQ, D_MODEL),
            new_k.reshape(cache_shape), new_v.reshape(cache_shape),
            jnp.stack(new_lru, axis=0).transpose(1, 3, 0, 2, 4).reshape(BATCH, DEPTH, 2, D_RNN),
            final_states(new_sr), final_states(new_si))
```
